```python
import math
import jax, jax.numpy as jnp
from jax import lax
import numpy as np

D_MODEL = 1024
BATCH = 1
SEQ = 16384
DEPTH = 2

CHUNK = 64
EPS = 1e-6
F32 = jnp.float32

SC_WIDTH = 512
SC_KERNEL = 3
SSM_D_INNER = 1024
SSM_HEAD_DIM = 64
SSM_HEADS = SSM_D_INNER // SSM_HEAD_DIM
SSM_GROUPS = 4
SSM_STATE = 128
SSM_CONV = 4
SSM_CONV_DIM = SSM_D_INNER + 2 * SSM_GROUPS * SSM_STATE
GM_WIDTH = 512
GM_GROUPS = 4
GM_BLOCK = 128
GM_GROUP_DIM = GM_WIDTH // GM_GROUPS
POOL_WIDTH = 512
POOL_WINDOWS = (2, 4, 8, 16)
POOL_GROUP = POOL_WIDTH // len(POOL_WINDOWS)
N_BRANCH = 4
IN_A = 3 * SC_WIDTH
IN_B = SSM_D_INNER + SSM_CONV_DIM + SSM_HEADS
IN_C = 2 * GM_WIDTH
IN_D = POOL_WIDTH
IN_TOTAL = IN_A + IN_B + IN_C + IN_D
D_FF = 2816
N_EXPERTS = 8
TOP_K = 2
D_FF_EXPERT = 3584
N_DENSE = (DEPTH + 1) // 2
N_MOE = DEPTH // 2

kernel_name = "hybrid_parallel_gated_mixers_moe"


def rmsnorm(x, g):
    xf = x.astype(F32)
    y = xf * lax.rsqrt(jnp.mean(xf * xf, -1, keepdims=True) + EPS)
    return (y * g.astype(F32)).astype(x.dtype)


def causal_dwconv(x, w):
    k = w.shape[0]
    L = x.shape[1]
    xp = jnp.pad(x, ((0, 0), (k - 1, 0), (0, 0)))
    return sum(xp[:, i:i + L] * w[i] for i in range(k))


def short_conv_mixer(h, b_gate, c_gate, conv_w):
    return b_gate * causal_dwconv(c_gate * h, conv_w)


def segsum_exp(a):
    cs = jnp.cumsum(a, -1)
    diff = cs[..., :, None] - cs[..., None, :]
    l = a.shape[-1]
    mask = jnp.tril(jnp.ones((l, l), bool))
    return jnp.exp(jnp.where(mask, diff, -jnp.inf))


def ssd_scan(x, dt, a, bmat, cmat):
    bsz, L, H, P = x.shape
    G, N = bmat.shape[-2:]
    K = H // G
    nc = L // CHUNK
    xdt = (x.astype(F32) * dt[..., None]).reshape(bsz, nc, CHUNK, G, K, P)
    da = jnp.moveaxis((dt * a).reshape(bsz, nc, CHUNK, G, K), 2, -1)
    bc = bmat.astype(F32).reshape(bsz, nc, CHUNK, G, N)
    cc = cmat.astype(F32).reshape(bsz, nc, CHUNK, G, N)
    da_cs = jnp.cumsum(da, -1)
    cb = jnp.einsum("bclgn,bcsgn->bcgls", cc, bc)
    scores = cb[:, :, :, None] * segsum_exp(da)
    y_diag = jnp.einsum("bcgkls,bcsgkp->bclgkp", scores, xdt)
    decay_to_end = jnp.exp(da_cs[..., -1:] - da_cs)
    states = jnp.einsum("bclgn,bcgkl,bclgkp->bcgkpn", bc, decay_to_end, xdt)
    chunk_decay = jnp.exp(da_cs[..., -1])

    def step(h, inp):
        s_c, d_c = inp
        return h * d_c[..., None, None] + s_c, h

    h0 = jnp.zeros((bsz, G, K, P, N), F32)
    _, h_in = lax.scan(step, h0, (jnp.moveaxis(states, 1, 0), jnp.moveaxis(chunk_decay, 1, 0)))
    h_in = jnp.moveaxis(h_in, 0, 1)
    y_off = jnp.einsum("bclgn,bcgkpn,bcgkl->bclgkp", cc, h_in, jnp.exp(da_cs))
    return (y_diag + y_off).reshape(bsz, L, H, P)


def mamba2_mixer(z, xbc, dt_raw, conv_w, conv_b, dt_bias, a_log, d_skip, norm_g):
    bsz, L, _ = z.shape
    xbc = jax.nn.silu(causal_dwconv(xbc, conv_w) + conv_b)
    xs, bm, cm = jnp.split(xbc, [SSM_D_INNER, SSM_D_INNER + SSM_GROUPS * SSM_STATE], -1)
    xs = xs.reshape(bsz, L, SSM_HEADS, SSM_HEAD_DIM)
    bm = bm.reshape(bsz, L, SSM_GROUPS, SSM_STATE)
    cm = cm.reshape(bsz, L, SSM_GROUPS, SSM_STATE)
    dt = jax.nn.softplus(dt_raw.astype(F32) + dt_bias.astype(F32))
    a = -jnp.exp(a_log.astype(F32))
    y = ssd_scan(xs, dt, a, bm, cm) + xs.astype(F32) * d_skip.astype(F32)[:, None]
    y = y.reshape(bsz, L, SSM_D_INNER) * jax.nn.silu(z.astype(F32))
    yg = y.reshape(bsz, L, SSM_GROUPS, -1)
    yg = yg * lax.rsqrt(jnp.mean(yg * yg, -1, keepdims=True) + EPS)
    return (yg.reshape(bsz, L, SSM_D_INNER) * norm_g.astype(F32)).astype(z.dtype)


def gmlp_mixer(uv, ln_g, ln_b, w_s, s_bias):
    uv = jax.nn.gelu(uv)
    u, v = jnp.split(uv, 2, -1)
    vf = v.astype(F32)
    mu = jnp.mean(vf, -1, keepdims=True)
    var = jnp.mean(jnp.square(vf - mu), -1, keepdims=True)
    vf = (vf - mu) * lax.rsqrt(var + EPS) * ln_g.astype(F32) + ln_b.astype(F32)
    bsz, L, _ = v.shape
    nb = L // GM_BLOCK
    vf = vf.reshape(bsz, nb, GM_BLOCK, GM_GROUPS, GM_GROUP_DIM)
    mask = jnp.tril(jnp.ones((GM_BLOCK, GM_BLOCK), bool))
    ws = jnp.where(mask, w_s.astype(F32), 0.0)
    s = jnp.einsum("gts,bnsgc->bntgc", ws, vf) + s_bias.astype(F32).T[None, None, :, :, None]
    return u * s.reshape(bsz, L, GM_WIDTH).astype(u.dtype)


def pool_mixer(p, w_map, scale):
    bsz, L, _ = p.shape
    pf = p.astype(F32)
    count_base = jnp.arange(1, L + 1, dtype=F32)[None, :, None]
    outs = []
    for gi, w in enumerate(POOL_WINDOWS):
        xg = pf[..., gi * POOL_GROUP:(gi + 1) * POOL_GROUP]
        cs = jnp.cumsum(xg, 1)
        lag = jnp.pad(cs[:, :-w], ((0, 0), (w, 0), (0, 0)))
        outs.append((cs - lag) / jnp.minimum(count_base, w) - xg)
    pooled = jnp.stack(outs, 2)
    y = jnp.einsum("blgc,gcd->blgd", pooled, w_map.astype(F32)).reshape(bsz, L, POOL_WIDTH)
    return (y * scale.astype(F32)).astype(p.dtype)


def swiglu(x, wg, wu, wd):
    return (jax.nn.silu(x @ wg) * (x @ wu)) @ wd


def moe_ffn(x, router_w, wg, wu, wd):
    bsz, L, D = x.shape
    xt = x.reshape(-1, D)
    logits = (xt @ router_w).astype(F32)
    top_vals, top_idx = lax.top_k(logits, TOP_K)
    gates = jax.nn.softmax(top_vals, -1)
    combine = jnp.sum(jax.nn.one_hot(top_idx, N_EXPERTS, dtype=F32) * gates[..., None], 1)
    out = jnp.zeros(xt.shape, F32)
    for e in range(N_EXPERTS):
        out = out + combine[:, e:e + 1] * swiglu(xt, wg[e], wu[e], wd[e]).astype(F32)
    return out.astype(x.dtype).reshape(bsz, L, D)


def setup_inputs(seed: int = 0) -> dict:
    key = jax.random.key(seed)
    ks = iter(jax.random.split(key, 48))
    D = D_MODEL

    def nrm(shape, scale):
        return jax.random.normal(next(ks), shape, F32) * scale

    def gain(shape):
        return 1.0 + nrm(shape, 0.05)

    x = nrm((BATCH, SEQ, D), 1.0)
    norm1_g = gain((DEPTH, D))
    w_in = nrm((DEPTH, D, IN_TOTAL), D ** -0.5)
    w_gate = nrm((DEPTH, D, N_BRANCH * D), D ** -0.5)
    sc_conv_w = nrm((DEPTH, SC_KERNEL, SC_WIDTH), SC_KERNEL ** -0.5)
    ssm_conv_w = nrm((DEPTH, SSM_CONV, SSM_CONV_DIM), SSM_CONV ** -0.5)
    ssm_conv_b = nrm((DEPTH, SSM_CONV_DIM), 0.01)
    dt0 = jnp.exp(jax.random.uniform(next(ks), (DEPTH, SSM_HEADS), F32, math.log(1e-3), math.log(1e-1)))
    ssm_dt_bias = dt0 + jnp.log(-jnp.expm1(-dt0))
    ssm_a_log = jnp.log(jax.random.uniform(next(ks), (DEPTH, SSM_HEADS), F32, 1.0, 16.0))
    ssm_d = gain((DEPTH, SSM_HEADS))
    ssm_norm_g = gain((DEPTH, SSM_D_INNER))
    gm_ln_g = gain((DEPTH, GM_WIDTH))
    gm_ln_b = nrm((DEPTH, GM_WIDTH), 0.01)
    gm_ws = nrm((DEPTH, GM_GROUPS, GM_BLOCK, GM_BLOCK), GM_BLOCK ** -0.5)
    gm_bias = gain((DEPTH, GM_GROUPS, GM_BLOCK))
    pool_map = nrm((DEPTH, len(POOL_WINDOWS), POOL_GROUP, POOL_GROUP), POOL_GROUP ** -0.5)
    pool_scale = gain((DEPTH, POOL_WIDTH))
    w_br_a = nrm((DEPTH, SC_WIDTH, D), SC_WIDTH ** -0.5)
    w_br_b = nrm((DEPTH, SSM_D_INNER, D), SSM_D_INNER ** -0.5)
    w_br_c = nrm((DEPTH, GM_WIDTH, D), GM_WIDTH ** -0.5)
    w_br_d = nrm((DEPTH, POOL_WIDTH, D), POOL_WIDTH ** -0.5)
    w_out = nrm((DEPTH, D, D), D ** -0.5)
    norm2_g = gain((DEPTH, D))
    ffn_wg = nrm((N_DENSE, D, D_FF), D ** -0.5)
    ffn_wu = nrm((N_DENSE, D, D_FF), D ** -0.5)
    ffn_wd = nrm((N_DENSE, D_FF, D), D_FF ** -0.5)
    router_w = nrm((N_MOE, D, N_EXPERTS), D ** -0.5)
    moe_wg = nrm((N_MOE, N_EXPERTS, D, D_FF_EXPERT), D ** -0.5)
    moe_wu = nrm((N_MOE, N_EXPERTS, D, D_FF_EXPERT), D ** -0.5)
    moe_wd = nrm((N_MOE, N_EXPERTS, D_FF_EXPERT, D), D_FF_EXPERT ** -0.5)
    final_g = gain((D,))
    return {"x": x, "norm1_g": norm1_g, "w_in": w_in, "w_gate": w_gate,
            "sc_conv_w": sc_conv_w, "ssm_conv_w": ssm_conv_w, "ssm_conv_b": ssm_conv_b,
            "ssm_dt_bias": ssm_dt_bias, "ssm_a_log": ssm_a_log, "ssm_d": ssm_d,
            "ssm_norm_g": ssm_norm_g, "gm_ln_g": gm_ln_g, "gm_ln_b": gm_ln_b,
            "gm_ws": gm_ws, "gm_bias": gm_bias, "pool_map": pool_map,
            "pool_scale": pool_scale, "w_br_a": w_br_a, "w_br_b": w_br_b,
            "w_br_c": w_br_c, "w_br_d": w_br_d, "w_out": w_out, "norm2_g": norm2_g,
            "ffn_wg": ffn_wg, "ffn_wu": ffn_wu, "ffn_wd": ffn_wd, "router_w": router_w,
            "moe_wg": moe_wg, "moe_wu": moe_wu, "moe_wd": moe_wd, "final_g": final_g}


def reference(x, norm1_g, w_in, w_gate, sc_conv_w, ssm_conv_w, ssm_conv_b, ssm_dt_bias,
              ssm_a_log, ssm_d, ssm_norm_g, gm_ln_g, gm_ln_b, gm_ws, gm_bias, pool_map,
              pool_scale, w_br_a, w_br_b, w_br_c, w_br_d, w_out, norm2_g, ffn_wg, ffn_wu,
              ffn_wd, router_w, moe_wg, moe_wu, moe_wd, final_g):
    bsz, L, D = x.shape
    for layer in range(DEPTH):
        h = rmsnorm(x, norm1_g[layer])
        proj = h @ w_in[layer]
        pa, pb, pc, pd = jnp.split(proj, [IN_A, IN_A + IN_B, IN_A + IN_B + IN_C], -1)
        a_h, a_b, a_c = jnp.split(pa, 3, -1)
        y_a = short_conv_mixer(a_h, a_b, a_c, sc_conv_w[layer])
        z, xbc, dt_raw = jnp.split(pb, [SSM_D_INNER, SSM_D_INNER + SSM_CONV_DIM], -1)
        y_b = mamba2_mixer(z, xbc, dt_raw, ssm_conv_w[layer], ssm_conv_b[layer],
                           ssm_dt_bias[layer], ssm_a_log[layer], ssm_d[layer], ssm_norm_g[layer])
        y_c = gmlp_mixer(pc, gm_ln_g[layer], gm_ln_b[layer], gm_ws[layer], gm_bias[layer])
        y_d = pool_mixer(pd, pool_map[layer], pool_scale[layer])
        gates = jax.nn.sigmoid((h @ w_gate[layer]).astype(F32)).reshape(bsz, L, N_BRANCH, D)
        merged = (gates[:, :, 0] * (y_a @ w_br_a[layer]).astype(F32)
                  + gates[:, :, 1] * (y_b @ w_br_b[layer]).astype(F32)
                  + gates[:, :, 2] * (y_c @ w_br_c[layer]).astype(F32)
                  + gates[:, :, 3] * (y_d @ w_br_d[layer]).astype(F32))
        x = x + merged.astype(x.dtype) @ w_out[layer]
        h2 = rmsnorm(x, norm2_g[layer])
        idx = layer // 2
        if layer % 2 == 0:
            f = swiglu(h2, ffn_wg[idx], ffn_wu[idx], ffn_wd[idx])
        else:
            f = moe_ffn(h2, router_w[idx], moe_wg[idx], moe_wu[idx], moe_wd[idx])
        x = x + f
    return rmsnorm(x, final_g)
```

```python
import functools

import numpy as np
import jax
import jax.numpy as jnp
from jax import lax
from jax.experimental import pallas as pl
from jax.experimental.pallas import tpu as pltpu

F32 = jnp.float32
BF16 = jnp.bfloat16
EPS = 1e-6

LANES = 128
SUBLANES = 8

D_MODEL = 1024
SC_WIDTH = 512
SSM_D_INNER = 1024
SSM_HEAD_DIM = 64
SSM_HEADS = 16
SSM_GROUPS = 4
SSM_STATE = 128
SSM_CONV_DIM = 2048
GM_WIDTH = 512
GM_GROUPS = 4
GM_BLOCK = 128
POOL_WIDTH = 512
POOL_WINDOWS = (2, 4, 8, 16)
POOL_HALO = 16
N_BRANCH = 4
N_EXPERTS = 8
TOP_K = 2

P_AH, P_AB, P_AC = 0, 512, 1024
P_Z = 1536
P_XBC = 2560
P_DT = 4608
P_U = P_DT + LANES
P_V = P_U + GM_WIDTH
P_PD = P_V + GM_WIDTH
P_TOTAL = P_PD + POOL_WIDTH
IN_DT_END = 4624

BR_TOTAL = SC_WIDTH + SSM_D_INNER + GM_WIDTH + POOL_WIDTH

TM_PROJ = 1024
TM_MIX = 256
TM_FFN = 512
TM_ROUTE = 256
TM_GROUP = 256
FFN_CHUNK = 1408
MOE_CHUNK = 512


def _vmem_limit(nbytes):
    return int(min(nbytes + (8 << 20), 60 << 20))


def _rms(x, g):
    return x * lax.rsqrt(jnp.mean(x * x, axis=-1, keepdims=True) + EPS) * g


def _silu(x):
    return x * jax.nn.sigmoid(x)


def _norm_matmul_kernel(x_ref, g_ref, w_ref, o_ref, h_ref, *, sigmoid):
    @pl.when(pl.program_id(1) == 0)
    def _():
        h_ref[...] = _rms(x_ref[...], g_ref[...]).astype(BF16)

    acc = jnp.dot(h_ref[...], w_ref[...], preferred_element_type=F32)
    if sigmoid:
        acc = jax.nn.sigmoid(acc)
    o_ref[...] = acc.astype(o_ref.dtype)


def _norm_matmul(x, g, w, *, tn, out_dtype, sigmoid):
    L, d = x.shape
    n = w.shape[1]
    tm = TM_PROJ
    est = 2 * (tm * d * 4 + d * tn * 2 + tm * tn * jnp.dtype(out_dtype).itemsize) + tm * d * 2 + 2 * tm * tn * 4
    return pl.pallas_call(
        functools.partial(_norm_matmul_kernel, sigmoid=sigmoid),
        grid=(L // tm, n // tn),
        in_specs=[
            pl.BlockSpec((tm, d), lambda i, j: (i, 0)),
            pl.BlockSpec((1, d), lambda i, j: (0, 0)),
            pl.BlockSpec((d, tn), lambda i, j: (0, j)),
        ],
        out_specs=pl.BlockSpec((tm, tn), lambda i, j: (i, j)),
        out_shape=jax.ShapeDtypeStruct((L, n), out_dtype),
        scratch_shapes=[pltpu.VMEM((tm, d), BF16)],
        compiler_params=pltpu.CompilerParams(
            dimension_semantics=("parallel", "arbitrary"), vmem_limit_bytes=_vmem_limit(est)),
        name="norm_proj_sigmoid" if sigmoid else "norm_proj",
    )(x, g, w)


def _shift_rows(ext, k, halo, tm):
    return pltpu.roll(ext, k, 0)[halo:halo + tm]


def _split3(x):
    hi = x.astype(BF16)
    r1 = x - hi.astype(F32)
    mid = r1.astype(BF16)
    lo = (r1 - mid.astype(F32)).astype(BF16)
    return hi, mid, lo


def _mixer_kernel(proj_ref, gates_ref, x_ref, scw_ref, mcw_ref, mcb_ref, dtb_ref, alog_ref,
                  dskip_ref, ng_ref, lng_ref, lnb_ref, gws_ref, gbias_ref, pmap_ref, pscale_ref,
                  rexp_ref, wbr_ref, wout_ref, o_ref,
                  halo_a, halo_x, halo_p, state, xbc_s, y_s):
    i = pl.program_id(0)
    tm = x_ref.shape[0]

    @pl.when(i == 0)
    def _init():
        halo_a[...] = jnp.zeros_like(halo_a)
        halo_x[...] = jnp.zeros_like(halo_x)
        halo_p[...] = jnp.zeros_like(halo_p)
        state[...] = jnp.zeros_like(state)

    row = lax.broadcasted_iota(jnp.int32, (tm, 1), 0)
    lane = lax.broadcasted_iota(jnp.int32, (1, LANES), 1)

    ch = proj_ref[:, P_AC:P_AC + SC_WIDTH] * proj_ref[:, P_AH:P_AH + SC_WIDTH]
    ext = jnp.concatenate([halo_a[...], ch], axis=0)
    scw = scw_ref[...]
    conv_a = (ch * scw[2:3, :]
              + _shift_rows(ext, 1, SUBLANES, tm) * scw[1:2, :]
              + _shift_rows(ext, 2, SUBLANES, tm) * scw[0:1, :])
    halo_a[...] = ch[tm - SUBLANES:tm, :]
    y_s[:, 0:SC_WIDTH] = (proj_ref[:, P_AB:P_AB + SC_WIDTH] * conv_a).astype(BF16)

    cw = 512
    for c in range(SSM_CONV_DIM // cw):
        xc = proj_ref[:, P_XBC + c * cw:P_XBC + (c + 1) * cw]
        extx = jnp.concatenate([halo_x[:, c * cw:(c + 1) * cw], xc], axis=0)
        mcw = mcw_ref[:, c * cw:(c + 1) * cw]
        conv = (xc * mcw[3:4, :]
                + _shift_rows(extx, 1, SUBLANES, tm) * mcw[2:3, :]
                + _shift_rows(extx, 2, SUBLANES, tm) * mcw[1:2, :]
                + _shift_rows(extx, 3, SUBLANES, tm) * mcw[0:1, :]
                + mcb_ref[:, c * cw:(c + 1) * cw])
        halo_x[:, c * cw:(c + 1) * cw] = xc[tm - SUBLANES:tm, :]
        xbc_s[:, c * cw:(c + 1) * cw] = _silu(conv)

    dt_in = proj_ref[:, P_DT:P_DT + LANES] + dtb_ref[...]
    dt = jnp.maximum(dt_in, 0.0) + jnp.log1p(jnp.exp(-jnp.abs(dt_in)))
    da = dt * (-jnp.exp(alog_ref[...]))
    col = lax.broadcasted_iota(jnp.int32, (1, tm), 1)
    causal = row >= col
    cs = jnp.dot(causal.astype(F32), da, precision=lax.Precision.HIGHEST,
                 preferred_element_type=F32)
    head_lane = lane < SSM_HEADS
    decay_in = jnp.exp(cs)
    to_end = jnp.exp(cs[tm - 1:tm, :] - cs) * dt
    packed = (jnp.where(head_lane, decay_in, 0.0)
              + pltpu.roll(jnp.where(head_lane, to_end, 0.0), SSM_HEADS, 1))
    hi, mid, lo = _split3(packed)
    x3 = (hi.astype(F32) + pltpu.roll(mid.astype(F32), 2 * SSM_HEADS, 1)
          + pltpu.roll(lo.astype(F32), 4 * SSM_HEADS, 1)).astype(BF16)
    expanded = jnp.dot(x3, rexp_ref[...], preferred_element_type=F32)
    decay_cols = expanded[:, 0:SSM_D_INNER]
    toend_cols = expanded[:, SSM_D_INNER:2 * SSM_D_INNER]

    tpk = jnp.where(head_lane, cs, 0.0) + pltpu.roll(jnp.where(head_lane, dt, 0.0), SSM_HEADS, 1)
    tpk_t = tpk.T

    gw = SSM_D_INNER // SSM_GROUPS
    lo_half = lane < SSM_HEAD_DIM
    for g in range(SSM_GROUPS):
        b_g = xbc_s[:, SSM_D_INNER + g * SSM_STATE:SSM_D_INNER + (g + 1) * SSM_STATE].astype(BF16)
        c_off = SSM_D_INNER + SSM_GROUPS * SSM_STATE
        c_g = xbc_s[:, c_off + g * SSM_STATE:c_off + (g + 1) * SSM_STATE].astype(BF16)
        cb = lax.dot_general(c_g, b_g, (((1,), (1,)), ((), ())), preferred_element_type=F32)
        s_g = state[g * SSM_STATE:(g + 1) * SSM_STATE, :]
        y_off = jnp.dot(c_g, s_g.astype(BF16), preferred_element_type=F32) * decay_cols[:, g * gw:(g + 1) * gw]
        xs_g = xbc_s[:, g * gw:(g + 1) * gw]
        pieces = []
        for jj in range(2):
            xs_t = xs_g[:, jj * LANES:(jj + 1) * LANES].astype(BF16)
            acc = None
            for kk in range(2):
                h = g * 4 + jj * 2 + kk
                seg = jnp.exp(jnp.where(causal, cs[:, h:h + 1] - tpk_t[h:h + 1, :], -jnp.inf))
                scores = (cb * seg * tpk_t[SSM_HEADS + h:SSM_HEADS + h + 1, :]).astype(BF16)
                keep = lo_half if kk == 0 else jnp.logical_not(lo_half)
                part = jnp.dot(scores, jnp.where(keep, xs_t, jnp.zeros_like(xs_t)),
                               preferred_element_type=F32)
                acc = part if acc is None else acc + part
            pieces.append(acc)
        y_g = jnp.concatenate(pieces, axis=1) + y_off + xs_g * dskip_ref[:, g * gw:(g + 1) * gw]
        upd = jnp.dot(b_g.T, (toend_cols[:, g * gw:(g + 1) * gw] * xs_g).astype(BF16),
                      preferred_element_type=F32)
        state[g * SSM_STATE:(g + 1) * SSM_STATE, :] = decay_cols[tm - 1:tm, g * gw:(g + 1) * gw] * s_g + upd
        y_g = y_g * _silu(proj_ref[:, P_Z + g * gw:P_Z + (g + 1) * gw])
        y_g = y_g * lax.rsqrt(jnp.mean(y_g * y_g, axis=-1, keepdims=True) + EPS)
        y_s[:, SC_WIDTH + g * gw:SC_WIDTH + (g + 1) * gw] = (y_g * ng_ref[:, g * gw:(g + 1) * gw]).astype(BF16)

    def gelu(t):
        return 0.5 * t * (1.0 + jnp.tanh(0.7978845608028654 * (t + 0.044715 * (t * t * t))))

    v = gelu(proj_ref[:, P_V:P_V + GM_WIDTH])
    mu = jnp.mean(v, axis=-1, keepdims=True)
    vc = v - mu
    var = jnp.mean(vc * vc, axis=-1, keepdims=True)
    vf = (vc * lax.rsqrt(var + EPS) * lng_ref[...] + lnb_ref[...]).astype(BF16)
    r_b = lax.broadcasted_iota(jnp.int32, (GM_BLOCK, 1), 0)
    c_b = lax.broadcasted_iota(jnp.int32, (1, GM_BLOCK), 1)
    gc = GM_WIDTH // GM_GROUPS
    for g in range(GM_GROUPS):
        wsm = jnp.where(r_b >= c_b, gws_ref[g], 0.0).astype(BF16)
        blocks = [jnp.dot(wsm, vf[b * GM_BLOCK:(b + 1) * GM_BLOCK, g * gc:(g + 1) * gc],
                          preferred_element_type=F32) + gbias_ref[:, g * gc:(g + 1) * gc]
                  for b in range(tm // GM_BLOCK)]
        s_sp = jnp.concatenate(blocks, axis=0)
        u_g = gelu(proj_ref[:, P_U + g * gc:P_U + (g + 1) * gc])
        off = SC_WIDTH + SSM_D_INNER
        y_s[:, off + g * gc:off + (g + 1) * gc] = (u_g * s_sp).astype(BF16)

    t_glob = i * tm + row
    pc = POOL_WIDTH // len(POOL_WINDOWS)
    for g, w in enumerate(POOL_WINDOWS):
        pd_g = proj_ref[:, P_PD + g * pc:P_PD + (g + 1) * pc]
        s = jnp.concatenate([halo_p[:, g * pc:(g + 1) * pc], pd_g], axis=0)
        k = 1
        while k < w:
            s = s + pltpu.roll(s, k, 0)
            k *= 2
        win = s[POOL_HALO:POOL_HALO + tm]
        inv_cnt = 1.0 / jnp.minimum(t_glob + 1, w).astype(F32)
        pooled = (win * inv_cnt - pd_g).astype(BF16)
        halo_p[:, g * pc:(g + 1) * pc] = pd_g[tm - POOL_HALO:tm, :]
        off = SC_WIDTH + SSM_D_INNER + GM_WIDTH
        y_d = jnp.dot(pooled, pmap_ref[g], preferred_element_type=F32) * pscale_ref[:, g * pc:(g + 1) * pc]
        y_s[:, off + g * pc:off + (g + 1) * pc] = y_d.astype(BF16)

    bounds = (0, SC_WIDTH, SC_WIDTH + SSM_D_INNER, SC_WIDTH + SSM_D_INNER + GM_WIDTH, BR_TOTAL)
    merged = None
    for b in range(N_BRANCH):
        br = jnp.dot(y_s[:, bounds[b]:bounds[b + 1]], wbr_ref[bounds[b]:bounds[b + 1], :],
                     preferred_element_type=F32)
        term = gates_ref[:, b * D_MODEL:(b + 1) * D_MODEL].astype(F32) * br
        merged = term if merged is None else merged + term
    o_ref[...] = x_ref[...] + jnp.dot(merged.astype(BF16), wout_ref[...], preferred_element_type=F32)


def _head_expand_matrix():
    r = np.zeros((LANES, 2 * SSM_D_INNER), np.float32)
    for piece in range(3):
        for q in range(2):
            for h in range(SSM_HEADS):
                rr = piece * 2 * SSM_HEADS + q * SSM_HEADS + h
                r[rr, q * SSM_D_INNER + h * SSM_HEAD_DIM:q * SSM_D_INNER + (h + 1) * SSM_HEAD_DIM] = 1.0
    return jnp.asarray(r, BF16)


def _mixer(proj, gates, x, p):
    L = x.shape[0]
    tm = TM_MIX
    full = lambda a: pl.BlockSpec(a.shape, lambda i, _n=a.ndim: (0,) * _n, pipeline_mode=pl.Buffered(1))
    small = [p["scw"], p["mcw"], p["mcb"], p["dtb"], p["alog"], p["dskip"], p["ng"], p["lng"],
             p["lnb"], p["gws"], p["gbias"], p["pmap"], p["pscale"], p["rexp"], p["wbr"], p["wout"]]
    est = (2 * (tm * P_TOTAL * 4 + tm * N_BRANCH * D_MODEL * 2 + 2 * tm * D_MODEL * 4)
           + sum(int(a.size) * a.dtype.itemsize for a in small)
           + tm * SSM_CONV_DIM * 4 + tm * BR_TOTAL * 2 + 16 * tm * D_MODEL * 4)
    return pl.pallas_call(
        _mixer_kernel,
        grid=(L // tm,),
        in_specs=[pl.BlockSpec((tm, P_TOTAL), lambda i: (i, 0)),
                  pl.BlockSpec((tm, N_BRANCH * D_MODEL), lambda i: (i, 0)),
                  pl.BlockSpec((tm, D_MODEL), lambda i: (i, 0))] + [full(a) for a in small],
        out_specs=pl.BlockSpec((tm, D_MODEL), lambda i: (i, 0)),
        out_shape=jax.ShapeDtypeStruct((L, D_MODEL), F32),
        scratch_shapes=[
            pltpu.VMEM((SUBLANES, SC_WIDTH), F32),
            pltpu.VMEM((SUBLANES, SSM_CONV_DIM), F32),
            pltpu.VMEM((POOL_HALO, POOL_WIDTH), F32),
            pltpu.VMEM((SSM_GROUPS * SSM_STATE, SSM_D_INNER // SSM_GROUPS), F32),
            pltpu.VMEM((tm, SSM_CONV_DIM), F32),
            pltpu.VMEM((tm, BR_TOTAL), BF16),
        ],
        compiler_params=pltpu.CompilerParams(
            dimension_semantics=("arbitrary",), vmem_limit_bytes=_vmem_limit(est)),
        name="mixers",
    )(proj, gates, x, *small)


def _ffn_kernel(x_ref, g_ref, wg_ref, wu_ref, wd_ref, o_ref):
    x = x_ref[...]
    h = _rms(x, g_ref[...]).astype(BF16)
    acc = x
    for c in range(wg_ref.shape[1] // FFN_CHUNK):
        sl = slice(c * FFN_CHUNK, (c + 1) * FFN_CHUNK)
        gate = jnp.dot(h, wg_ref[:, sl], preferred_element_type=F32)
        up = jnp.dot(h, wu_ref[:, sl], preferred_element_type=F32)
        acc = acc + jnp.dot((_silu(gate) * up).astype(BF16), wd_ref[sl, :], preferred_element_type=F32)
    o_ref[...] = acc


def _dense_ffn(x, g, wg, wu, wd):
    L, d = x.shape
    f = wg.shape[1]
    tm = TM_FFN
    est = 4 * tm * d * 4 + 3 * d * f * 2 + 4 * tm * FFN_CHUNK * 4
    return pl.pallas_call(
        _ffn_kernel,
        grid=(L // tm,),
        in_specs=[pl.BlockSpec((tm, d), lambda i: (i, 0)),
                  pl.BlockSpec((1, d), lambda i: (0, 0)),
                  pl.BlockSpec((d, f), lambda i: (0, 0), pipeline_mode=pl.Buffered(1)),
                  pl.BlockSpec((d, f), lambda i: (0, 0), pipeline_mode=pl.Buffered(1)),
                  pl.BlockSpec((f, d), lambda i: (0, 0), pipeline_mode=pl.Buffered(1))],
        out_specs=pl.BlockSpec((tm, d), lambda i: (i, 0)),
        out_shape=jax.ShapeDtypeStruct((L, d), F32),
        compiler_params=pltpu.CompilerParams(
            dimension_semantics=("parallel",), vmem_limit_bytes=_vmem_limit(est)),
        name="dense_swiglu",
    )(x, g, wg, wu, wd)


def _router_kernel(x_ref, g_ref, rw_ref, ri_ref, rg_ref, cnt_ref, carry):
    i = pl.program_id(0)
    tm = x_ref.shape[0]

    @pl.when(i == 0)
    def _():
        carry[...] = jnp.zeros_like(carry)

    h = _rms(x_ref[...], g_ref[...])
    logits = jnp.dot(h, rw_ref[...], precision=lax.Precision.HIGHEST, preferred_element_type=F32)
    lane = lax.broadcasted_iota(jnp.int32, (1, LANES), 1).astype(F32)
    logits = jnp.where(lane < N_EXPERTS, logits, -jnp.inf)
    m1 = jnp.max(logits, axis=-1, keepdims=True)
    i1 = jnp.min(jnp.where(logits == m1, lane, float(LANES)), axis=-1, keepdims=True)
    rest = jnp.where(lane == i1, -jnp.inf, logits)
    m2 = jnp.max(rest, axis=-1, keepdims=True)
    i2 = jnp.min(jnp.where(rest == m2, lane, float(LANES)), axis=-1, keepdims=True)
    e21 = jnp.exp(m2 - m1)
    g1 = 1.0 / (1.0 + e21)
    g2 = e21 / (1.0 + e21)

    hot1 = lane == i1
    hot2 = lane == i2
    onehot = jnp.logical_or(hot1, hot2).astype(BF16)
    row = lax.broadcasted_iota(jnp.int32, (tm, 1), 0)
    colt = lax.broadcasted_iota(jnp.int32, (1, tm), 1)
    before = (row > colt).astype(BF16)
    seen = jnp.dot(before, onehot, preferred_element_type=F32) + carry[...]
    r1 = jnp.sum(jnp.where(hot1, seen, 0.0), axis=-1, keepdims=True)
    r2 = jnp.sum(jnp.where(hot2, seen, 0.0), axis=-1, keepdims=True)
    carry[...] = carry[...] + jnp.sum(onehot.astype(F32), axis=0, keepdims=True)

    ri = jnp.where(lane == 0, i1, jnp.where(lane == 1, i2, jnp.where(lane == 2, r1, r2)))
    ri_ref[...] = ri.astype(jnp.int32)
    rg_ref[...] = jnp.where(lane == 0, g1, g2)
    cnt_ref[...] = carry[...]


def _router(x, g, rw):
    L, d = x.shape
    tm = TM_ROUTE
    return pl.pallas_call(
        _router_kernel,
        grid=(L // tm,),
        in_specs=[pl.BlockSpec((tm, d), lambda i: (i, 0)),
                  pl.BlockSpec((1, d), lambda i: (0, 0)),
                  pl.BlockSpec((d, LANES), lambda i: (0, 0))],
        out_specs=[pl.BlockSpec((tm, LANES), lambda i: (i, 0)),
                   pl.BlockSpec((tm, LANES), lambda i: (i, 0)),
                   pl.BlockSpec((1, LANES), lambda i: (0, 0))],
        out_shape=[jax.ShapeDtypeStruct((L, LANES), jnp.int32),
                   jax.ShapeDtypeStruct((L, LANES), F32),
                   jax.ShapeDtypeStruct((1, LANES), F32)],
        scratch_shapes=[pltpu.VMEM((1, LANES), F32)],
        compiler_params=pltpu.CompilerParams(dimension_semantics=("arbitrary",)),
        name="router_top2",
    )(x, g, rw)


def _row_copy(src, dst, sem, s, d):
    return pltpu.make_async_copy(src.at[pl.ds(s, 1), :], dst.at[pl.ds(d, 1), :], sem)


def _dispatch_kernel(pos_ref, x_ref, g_ref, xs_in_ref, xs_ref, h_s, sem):
    del xs_in_ref
    tm = x_ref.shape[0]
    h_s[...] = _rms(x_ref[...], g_ref[...])

    def start(t, c):
        for k in range(TOP_K):
            _row_copy(h_s, xs_ref, sem, t, pos_ref[0, 0, TOP_K * t + k]).start()
        return c

    lax.fori_loop(0, tm, start, 0)

    def wait(t, c):
        for k in range(TOP_K):
            _row_copy(h_s, xs_ref, sem, t, pos_ref[0, 0, TOP_K * t + k]).wait()
        return c

    lax.fori_loop(0, tm, wait, 0)


def _dispatch(x, g, pos, n_sorted):
    L, d = x.shape
    tm = TM_ROUTE
    pos3 = pos.reshape(L // tm, 1, TOP_K * tm)
    xs0 = jnp.zeros((n_sorted, d), F32)
    return pl.pallas_call(
        _dispatch_kernel,
        grid=(L // tm,),
        in_specs=[pl.BlockSpec((1, 1, TOP_K * tm), lambda i: (i, 0, 0), memory_space=pltpu.SMEM),
                  pl.BlockSpec((tm, d), lambda i: (i, 0)),
                  pl.BlockSpec((1, d), lambda i: (0, 0)),
                  pl.BlockSpec(memory_space=pl.ANY)],
        out_specs=pl.BlockSpec(memory_space=pl.ANY),
        out_shape=jax.ShapeDtypeStruct((n_sorted, d), F32),
        scratch_shapes=[pltpu.VMEM((tm, d), F32), pltpu.SemaphoreType.DMA(())],
        input_output_aliases={3: 0},
        compiler_params=pltpu.CompilerParams(dimension_semantics=("arbitrary",), has_side_effects=True),
        name="moe_dispatch",
    )(pos3, x, g, xs0)


def _expert_kernel(te_ref, tv_ref, xs_ref, wg_ref, wu_ref, wd_ref, ys_ref):
    i = pl.program_id(0)

    @pl.when(tv_ref[i] == 0)
    def _():
        ys_ref[...] = jnp.zeros_like(ys_ref)

    @pl.when(tv_ref[i] != 0)
    def _():
        x = xs_ref[...].astype(BF16)
        acc = None
        for c in range(wg_ref.shape[2] // MOE_CHUNK):
            sl = slice(c * MOE_CHUNK, (c + 1) * MOE_CHUNK)
            gate = jnp.dot(x, wg_ref[0, :, sl], preferred_element_type=F32)
            up = jnp.dot(x, wu_ref[0, :, sl], preferred_element_type=F32)
            part = jnp.dot((_silu(gate) * up).astype(BF16), wd_ref[0, sl, :], preferred_element_type=F32)
            acc = part if acc is None else acc + part
        ys_ref[...] = acc


def _experts(xs, tile_expert, tile_valid, wg, wu, wd):
    n_sorted, d = xs.shape
    f = wg.shape[2]
    tm = TM_GROUP
    single = pl.Buffered(1)
    est = 3 * d * f * 2 + 4 * tm * d * 4 + 4 * tm * MOE_CHUNK * 4
    return pl.pallas_call(
        _expert_kernel,
        grid_spec=pltpu.PrefetchScalarGridSpec(
            num_scalar_prefetch=2,
            grid=(n_sorted // tm,),
            in_specs=[pl.BlockSpec((tm, d), lambda i, te, tv: (i, 0)),
                      pl.BlockSpec((1, d, f), lambda i, te, tv: (te[i], 0, 0), pipeline_mode=single),
                      pl.BlockSpec((1, d, f), lambda i, te, tv: (te[i], 0, 0), pipeline_mode=single),
                      pl.BlockSpec((1, f, d), lambda i, te, tv: (te[i], 0, 0), pipeline_mode=single)],
            out_specs=pl.BlockSpec((tm, d), lambda i, te, tv: (i, 0)),
        ),
        out_shape=jax.ShapeDtypeStruct((n_sorted, d), F32),
        compiler_params=pltpu.CompilerParams(
            dimension_semantics=("arbitrary",), vmem_limit_bytes=_vmem_limit(est)),
        name="expert_swiglu",
    )(tile_expert, tile_valid, xs, wg, wu, wd)


def _combine_kernel(pos_ref, x_ref, rg_ref, fg_ref, ys_ref, o_ref, buf, sem):
    tm = x_ref.shape[0]

    def start(t, c):
        for k in range(TOP_K):
            _row_copy(ys_ref, buf.at[k], sem, pos_ref[0, 0, TOP_K * t + k], t).start()
        return c

    lax.fori_loop(0, tm, start, 0)

    def wait(t, c):
        for k in range(TOP_K):
            _row_copy(ys_ref, buf.at[k], sem, pos_ref[0, 0, TOP_K * t + k], t).wait()
        return c

    lax.fori_loop(0, tm, wait, 0)

    rg = rg_ref[...]
    y = x_ref[...] + rg[:, 0:1] * buf[0] + rg[:, 1:2] * buf[1]
    o_ref[...] = _rms(y, fg_ref[...])


def _combine(x, rg, pos, ys, final_g):
    L, d = x.shape
    tm = TM_ROUTE
    pos3 = pos.reshape(L // tm, 1, TOP_K * tm)
    return pl.pallas_call(
        _combine_kernel,
        grid=(L // tm,),
        in_specs=[pl.BlockSpec((1, 1, TOP_K * tm), lambda i: (i, 0, 0), memory_space=pltpu.SMEM),
                  pl.BlockSpec((tm, d), lambda i: (i, 0)),
                  pl.BlockSpec((tm, LANES), lambda i: (i, 0)),
                  pl.BlockSpec((1, d), lambda i: (0, 0)),
                  pl.BlockSpec(memory_space=pl.ANY)],
        out_specs=pl.BlockSpec((tm, d), lambda i: (i, 0)),
        out_shape=jax.ShapeDtypeStruct((L, d), F32),
        scratch_shapes=[pltpu.VMEM((TOP_K, tm, d), F32), pltpu.SemaphoreType.DMA(())],
        compiler_params=pltpu.CompilerParams(dimension_semantics=("arbitrary",)),
        name="moe_combine_norm",
    )(pos3, x, rg, final_g, ys)


def _moe_ffn_final_norm(x, g2, router_w, wg, wu, wd, final_g):
    L, d = x.shape
    rw = jnp.pad(router_w, ((0, 0), (0, LANES - N_EXPERTS)))
    ri, rg, cnt = _router(x, g2, rw)
    counts = cnt[0, :N_EXPERTS].astype(jnp.int32)
    padded = ((counts + TM_GROUP - 1) // TM_GROUP) * TM_GROUP
    ends = jnp.cumsum(padded)
    starts = ends - padded
    pos = starts[ri[:, 0:TOP_K]] + ri[:, TOP_K:2 * TOP_K]
    n_sorted = TOP_K * L + N_EXPERTS * TM_GROUP
    tile_row = jnp.arange(n_sorted // TM_GROUP, dtype=jnp.int32) * TM_GROUP
    tile_valid = (tile_row < ends[-1]).astype(jnp.int32)
    last_row = jnp.minimum(tile_row, ends[-1] - TM_GROUP)
    tile_expert = jnp.sum((ends[None, :] <= last_row[:, None]).astype(jnp.int32), axis=1)
    tile_expert = jnp.minimum(tile_expert, N_EXPERTS - 1)

    xs = _dispatch(x, g2, pos, n_sorted)
    ys = _experts(xs, tile_expert, tile_valid, wg.astype(BF16), wu.astype(BF16), wd.astype(BF16))
    return _combine(x, rg, pos, ys, final_g.reshape(1, d))


def _token_mixer_layer(x, p):
    proj = _norm_matmul(x, p["norm1"], p["w_in"], tn=896, out_dtype=F32, sigmoid=False)
    gates = _norm_matmul(x, p["norm1"], p["w_gate"], tn=1024, out_dtype=BF16, sigmoid=True)
    return _mixer(proj, gates, x, p)


def kernel(x, norm1_g, w_in, w_gate, sc_conv_w, ssm_conv_w, ssm_conv_b, ssm_dt_bias, ssm_a_log, ssm_d, ssm_norm_g, gm_ln_g, gm_ln_b, gm_ws, gm_bias, pool_map, pool_scale, w_br_a, w_br_b, w_br_c, w_br_d, w_out, norm2_g, ffn_wg, ffn_wu, ffn_wd, router_w, moe_wg, moe_wu, moe_wd, final_g):
    bsz, L, d = x.shape
    assert bsz == 1 and d == D_MODEL and L % TM_PROJ == 0
    depth = norm1_g.shape[0]
    assert depth == 2, "layer 0 uses the dense SwiGLU, layer 1 the expert SwiGLU followed by the final norm"
    xt = x.reshape(L, d)
    rexp = _head_expand_matrix()
    pad_h = LANES - SSM_HEADS
    for layer in range(depth):
        w_in_l = w_in[layer]
        p = {
            "norm1": norm1_g[layer].reshape(1, d),
            "w_in": jnp.concatenate(
                [w_in_l[:, :IN_DT_END], jnp.zeros((d, pad_h), F32), w_in_l[:, IN_DT_END:]], axis=1).astype(BF16),
            "w_gate": w_gate[layer].astype(BF16),
            "scw": sc_conv_w[layer],
            "mcw": ssm_conv_w[layer],
            "mcb": ssm_conv_b[layer].reshape(1, -1),
            "dtb": jnp.pad(ssm_dt_bias[layer], (0, pad_h)).reshape(1, LANES),
            "alog": jnp.pad(ssm_a_log[layer], (0, pad_h)).reshape(1, LANES),
            "dskip": jnp.repeat(ssm_d[layer], SSM_HEAD_DIM).reshape(1, SSM_D_INNER),
            "ng": ssm_norm_g[layer].reshape(1, -1),
            "lng": gm_ln_g[layer].reshape(1, -1),
            "lnb": gm_ln_b[layer].reshape(1, -1),
            "gws": gm_ws[layer],
            "gbias": jnp.repeat(gm_bias[layer].T, GM_WIDTH // GM_GROUPS, axis=1),
            "pmap": pool_map[layer].astype(BF16),
            "pscale": pool_scale[layer].reshape(1, -1),
            "rexp": rexp,
            "wbr": jnp.concatenate([w_br_a[layer], w_br_b[layer], w_br_c[layer], w_br_d[layer]], axis=0).astype(BF16),
            "wout": w_out[layer].astype(BF16),
        }
        xt = _token_mixer_layer(xt, p)
        idx = layer // 2
        g2 = norm2_g[layer].reshape(1, d)
        if layer % 2 == 0:
            xt = _dense_ffn(xt, g2, ffn_wg[idx].astype(BF16), ffn_wu[idx].astype(BF16), ffn_wd[idx].astype(BF16))
        else:
            xt = _moe_ffn_final_norm(xt, g2, router_w[idx], moe_wg[idx], moe_wu[idx], moe_wd[idx], final_g)
    return xt.reshape(bsz, L, d)
```

```python
import numpy as np
import jax
import jax.numpy as jnp
from jax import lax
from jax.experimental import pallas as pl
from jax.experimental.pallas import tpu as pltpu

F32 = jnp.float32
BF16 = jnp.bfloat16
EPS = 1e-6

LANES = 128
SUBLANES = 8

D_MODEL = 1024
SC_WIDTH = 512
SSM_D_INNER = 1024
SSM_HEAD_DIM = 64
SSM_HEADS = 16
SSM_GROUPS = 4
SSM_STATE = 128
SSM_CONV_DIM = 2048
GM_WIDTH = 512
GM_GROUPS = 4
GM_BLOCK = 128
POOL_WIDTH = 512
POOL_WINDOWS = (2, 4, 8, 16)
POOL_HALO = 16
N_BRANCH = 4
N_EXPERTS = 8
TOP_K = 2

P_AH, P_AB, P_AC = 0, 512, 1024
P_Z = 1536
P_XBC = 2560
P_DT = 4608
P_U = P_DT + LANES
P_V = P_U + GM_WIDTH
P_PD = P_V + GM_WIDTH
P_TOTAL = P_PD + POOL_WIDTH
IN_DT_END = 4624

BR_TOTAL = SC_WIDTH + SSM_D_INNER + GM_WIDTH + POOL_WIDTH

TM_MIX = 256
PROJ_CHUNK = 896
TM_FFN = 512
TM_ROUTE = 256
TM_GROUP = 256
FFN_CHUNK = 1408
MOE_CHUNK = 512


def _vmem_limit(nbytes):
    return int(min(nbytes + (8 << 20), 60 << 20))


def _rms(x, g):
    return x * lax.rsqrt(jnp.mean(x * x, axis=-1, keepdims=True) + EPS) * g


def _silu(x):
    return x * jax.nn.sigmoid(x)


def _shift_rows(ext, k, halo, tm):
    return pltpu.roll(ext, k, 0)[halo:halo + tm]


def _split3(x):
    hi = x.astype(BF16)
    r1 = x - hi.astype(F32)
    mid = r1.astype(BF16)
    lo = (r1 - mid.astype(F32)).astype(BF16)
    return hi, mid, lo


def _mixer_kernel(x_ref, n1_ref, win_ref, wgate_ref, scw_ref, mcw_ref, mcb_ref, dtb_ref, alog_ref,
                  dskip_ref, ng_ref, lng_ref, lnb_ref, gws_ref, gbias_ref, pmap_ref, pscale_ref,
                  rexp_ref, wbr_ref, wout_ref, o_ref,
                  halo_a, halo_x, halo_p, state, h_s, proj_ref, xbc_s, y_s):
    i = pl.program_id(0)
    tm = x_ref.shape[0]

    @pl.when(i == 0)
    def _init():
        halo_a[...] = jnp.zeros_like(halo_a)
        halo_x[...] = jnp.zeros_like(halo_x)
        halo_p[...] = jnp.zeros_like(halo_p)
        state[...] = jnp.zeros_like(state)

    h_s[...] = _rms(x_ref[...], n1_ref[...]).astype(BF16)
    for c in range(P_TOTAL // PROJ_CHUNK):
        sl = slice(c * PROJ_CHUNK, (c + 1) * PROJ_CHUNK)
        proj_ref[:, sl] = jnp.dot(h_s[...], win_ref[:, sl], preferred_element_type=F32)

    row = lax.broadcasted_iota(jnp.int32, (tm, 1), 0)
    lane = lax.broadcasted_iota(jnp.int32, (1, LANES), 1)

    ch = proj_ref[:, P_AC:P_AC + SC_WIDTH] * proj_ref[:, P_AH:P_AH + SC_WIDTH]
    ext = jnp.concatenate([halo_a[...], ch], axis=0)
    scw = scw_ref[...]
    conv_a = (ch * scw[2:3, :]
              + _shift_rows(ext, 1, SUBLANES, tm) * scw[1:2, :]
              + _shift_rows(ext, 2, SUBLANES, tm) * scw[0:1, :])
    halo_a[...] = ch[tm - SUBLANES:tm, :]
    y_s[:, 0:SC_WIDTH] = (proj_ref[:, P_AB:P_AB + SC_WIDTH] * conv_a).astype(BF16)

    cw = 512
    for c in range(SSM_CONV_DIM // cw):
        xc = proj_ref[:, P_XBC + c * cw:P_XBC + (c + 1) * cw]
        extx = jnp.concatenate([halo_x[:, c * cw:(c + 1) * cw], xc], axis=0)
        mcw = mcw_ref[:, c * cw:(c + 1) * cw]
        conv = (xc * mcw[3:4, :]
                + _shift_rows(extx, 1, SUBLANES, tm) * mcw[2:3, :]
                + _shift_rows(extx, 2, SUBLANES, tm) * mcw[1:2, :]
                + _shift_rows(extx, 3, SUBLANES, tm) * mcw[0:1, :]
                + mcb_ref[:, c * cw:(c + 1) * cw])
        halo_x[:, c * cw:(c + 1) * cw] = xc[tm - SUBLANES:tm, :]
        xbc_s[:, c * cw:(c + 1) * cw] = _silu(conv)

    dt_in = proj_ref[:, P_DT:P_DT + LANES] + dtb_ref[...]
    dt = jnp.maximum(dt_in, 0.0) + jnp.log1p(jnp.exp(-jnp.abs(dt_in)))
    da = dt * (-jnp.exp(alog_ref[...]))
    col = lax.broadcasted_iota(jnp.int32, (1, tm), 1)
    causal = row >= col
    cs = jnp.dot(causal.astype(F32), da, precision=lax.Precision.HIGHEST,
                 preferred_element_type=F32)
    head_lane = lane < SSM_HEADS
    decay_in = jnp.exp(cs)
    to_end = jnp.exp(cs[tm - 1:tm, :] - cs) * dt
    packed = (jnp.where(head_lane, decay_in, 0.0)
              + pltpu.roll(jnp.where(head_lane, to_end, 0.0), SSM_HEADS, 1))
    hi, mid, lo = _split3(packed)
    x3 = (hi.astype(F32) + pltpu.roll(mid.astype(F32), 2 * SSM_HEADS, 1)
          + pltpu.roll(lo.astype(F32), 4 * SSM_HEADS, 1)).astype(BF16)
    expanded = jnp.dot(x3, rexp_ref[...], preferred_element_type=F32)
    decay_cols = expanded[:, 0:SSM_D_INNER]
    toend_cols = expanded[:, SSM_D_INNER:2 * SSM_D_INNER]

    tpk = jnp.where(head_lane, cs, 0.0) + pltpu.roll(jnp.where(head_lane, dt, 0.0), SSM_HEADS, 1)
    tpk_t = tpk.T

    gw = SSM_D_INNER // SSM_GROUPS
    lo_half = lane < SSM_HEAD_DIM
    for g in range(SSM_GROUPS):
        b_g = xbc_s[:, SSM_D_INNER + g * SSM_STATE:SSM_D_INNER + (g + 1) * SSM_STATE].astype(BF16)
        c_off = SSM_D_INNER + SSM_GROUPS * SSM_STATE
        c_g = xbc_s[:, c_off + g * SSM_STATE:c_off + (g + 1) * SSM_STATE].astype(BF16)
        cb = lax.dot_general(c_g, b_g, (((1,), (1,)), ((), ())), preferred_element_type=F32)
        s_g = state[g * SSM_STATE:(g + 1) * SSM_STATE, :]
        y_off = jnp.dot(c_g, s_g.astype(BF16), preferred_element_type=F32) * decay_cols[:, g * gw:(g + 1) * gw]
        xs_g = xbc_s[:, g * gw:(g + 1) * gw]
        pieces = []
        for jj in range(2):
            xs_t = xs_g[:, jj * LANES:(jj + 1) * LANES].astype(BF16)
            acc = None
            for kk in range(2):
                h = g * 4 + jj * 2 + kk
                seg = jnp.exp(jnp.where(causal, cs[:, h:h + 1] - tpk_t[h:h + 1, :], -jnp.inf))
                scores = (cb * seg * tpk_t[SSM_HEADS + h:SSM_HEADS + h + 1, :]).astype(BF16)
                keep = lo_half if kk == 0 else jnp.logical_not(lo_half)
                part = jnp.dot(scores, jnp.where(keep, xs_t, jnp.zeros_like(xs_t)),
                               preferred_element_type=F32)
                acc = part if acc is None else acc + part
            pieces.append(acc)
        y_g = jnp.concatenate(pieces, axis=1) + y_off + xs_g * dskip_ref[:, g * gw:(g + 1) * gw]
        upd = jnp.dot(b_g.T, (toend_cols[:, g * gw:(g + 1) * gw] * xs_g).astype(BF16),
                      preferred_element_type=F32)
        state[g * SSM_STATE:(g + 1) * SSM_STATE, :] = decay_cols[tm - 1:tm, g * gw:(g + 1) * gw] * s_g + upd
        y_g = y_g * _silu(proj_ref[:, P_Z + g * gw:P_Z + (g + 1) * gw])
        y_g = y_g * lax.rsqrt(jnp.mean(y_g * y_g, axis=-1, keepdims=True) + EPS)
        y_s[:, SC_WIDTH + g * gw:SC_WIDTH + (g + 1) * gw] = (y_g * ng_ref[:, g * gw:(g + 1) * gw]).astype(BF16)

    def gelu(t):
        return 0.5 * t * (1.0 + jnp.tanh(0.7978845608028654 * (t + 0.044715 * (t * t * t))))

    v = gelu(proj_ref[:, P_V:P_V + GM_WIDTH])
    mu = jnp.mean(v, axis=-1, keepdims=True)
    vc = v - mu
    var = jnp.mean(vc * vc, axis=-1, keepdims=True)
    vf = (vc * lax.rsqrt(var + EPS) * lng_ref[...] + lnb_ref[...]).astype(BF16)
    r_b = lax.broadcasted_iota(jnp.int32, (GM_BLOCK, 1), 0)
    c_b = lax.broadcasted_iota(jnp.int32, (1, GM_BLOCK), 1)
    gc = GM_WIDTH // GM_GROUPS
    for g in range(GM_GROUPS):
        wsm = jnp.where(r_b >= c_b, gws_ref[g], 0.0).astype(BF16)
        blocks = [jnp.dot(wsm, vf[b * GM_BLOCK:(b + 1) * GM_BLOCK, g * gc:(g + 1) * gc],
                          preferred_element_type=F32) + gbias_ref[:, g * gc:(g + 1) * gc]
                  for b in range(tm // GM_BLOCK)]
        s_sp = jnp.concatenate(blocks, axis=0)
        u_g = gelu(proj_ref[:, P_U + g * gc:P_U + (g + 1) * gc])
        off = SC_WIDTH + SSM_D_INNER
        y_s[:, off + g * gc:off + (g + 1) * gc] = (u_g * s_sp).astype(BF16)

    t_glob = i * tm + row
    pc = POOL_WIDTH // len(POOL_WINDOWS)
    for g, w in enumerate(POOL_WINDOWS):
        pd_g = proj_ref[:, P_PD + g * pc:P_PD + (g + 1) * pc]
        s = jnp.concatenate([halo_p[:, g * pc:(g + 1) * pc], pd_g], axis=0)
        k = 1
        while k < w:
            s = s + pltpu.roll(s, k, 0)
            k *= 2
        win = s[POOL_HALO:POOL_HALO + tm]
        inv_cnt = 1.0 / jnp.minimum(t_glob + 1, w).astype(F32)
        pooled = (win * inv_cnt - pd_g).astype(BF16)
        halo_p[:, g * pc:(g + 1) * pc] = pd_g[tm - POOL_HALO:tm, :]
        off = SC_WIDTH + SSM_D_INNER + GM_WIDTH
        y_d = jnp.dot(pooled, pmap_ref[g], preferred_element_type=F32) * pscale_ref[:, g * pc:(g + 1) * pc]
        y_s[:, off + g * pc:off + (g + 1) * pc] = y_d.astype(BF16)

    bounds = (0, SC_WIDTH, SC_WIDTH + SSM_D_INNER, SC_WIDTH + SSM_D_INNER + GM_WIDTH, BR_TOTAL)
    merged = None
    for b in range(N_BRANCH):
        br = jnp.dot(y_s[:, bounds[b]:bounds[b + 1]], wbr_ref[bounds[b]:bounds[b + 1], :],
                     preferred_element_type=F32)
        gate = jax.nn.sigmoid(jnp.dot(h_s[...], wgate_ref[:, b * D_MODEL:(b + 1) * D_MODEL],
                                      preferred_element_type=F32))
        term = gate * br
        merged = term if merged is None else merged + term
    o_ref[...] = x_ref[...] + jnp.dot(merged.astype(BF16), wout_ref[...], preferred_element_type=F32)


def _head_expand_matrix():
    r = np.zeros((LANES, 2 * SSM_D_INNER), np.float32)
    for piece in range(3):
        for q in range(2):
            for h in range(SSM_HEADS):
                rr = piece * 2 * SSM_HEADS + q * SSM_HEADS + h
                r[rr, q * SSM_D_INNER + h * SSM_HEAD_DIM:q * SSM_D_INNER + (h + 1) * SSM_HEAD_DIM] = 1.0
    return jnp.asarray(r, BF16)


def _mixer(x, p):
    L = x.shape[0]
    tm = TM_MIX
    full = lambda a: pl.BlockSpec(a.shape, lambda i, _n=a.ndim: (0,) * _n, pipeline_mode=pl.Buffered(1))
    consts = [p["norm1"], p["w_in"], p["w_gate"], p["scw"], p["mcw"], p["mcb"], p["dtb"], p["alog"],
              p["dskip"], p["ng"], p["lng"], p["lnb"], p["gws"], p["gbias"], p["pmap"], p["pscale"],
              p["rexp"], p["wbr"], p["wout"]]
    est = (4 * tm * D_MODEL * 4 + sum(int(a.size) * a.dtype.itemsize for a in consts)
           + tm * (D_MODEL * 2 + P_TOTAL * 4 + SSM_CONV_DIM * 4 + BR_TOTAL * 2) + 8 * tm * D_MODEL * 4)
    return pl.pallas_call(
        _mixer_kernel,
        grid=(L // tm,),
        in_specs=[pl.BlockSpec((tm, D_MODEL), lambda i: (i, 0))] + [full(a) for a in consts],
        out_specs=pl.BlockSpec((tm, D_MODEL), lambda i: (i, 0)),
        out_shape=jax.ShapeDtypeStruct((L, D_MODEL), F32),
        scratch_shapes=[
            pltpu.VMEM((SUBLANES, SC_WIDTH), F32),
            pltpu.VMEM((SUBLANES, SSM_CONV_DIM), F32),
            pltpu.VMEM((POOL_HALO, POOL_WIDTH), F32),
            pltpu.VMEM((SSM_GROUPS * SSM_STATE, SSM_D_INNER // SSM_GROUPS), F32),
            pltpu.VMEM((tm, D_MODEL), BF16),
            pltpu.VMEM((tm, P_TOTAL), F32),
            pltpu.VMEM((tm, SSM_CONV_DIM), F32),
            pltpu.VMEM((tm, BR_TOTAL), BF16),
        ],
        compiler_params=pltpu.CompilerParams(
            dimension_semantics=("arbitrary",), vmem_limit_bytes=_vmem_limit(est)),
        name="mixers",
    )(x, *consts)


def _ffn_kernel(x_ref, g_ref, wg_ref, wu_ref, wd_ref, o_ref):
    x = x_ref[...]
    h = _rms(x, g_ref[...]).astype(BF16)
    acc = x
    for c in range(wg_ref.shape[1] // FFN_CHUNK):
        sl = slice(c * FFN_CHUNK, (c + 1) * FFN_CHUNK)
        gate = jnp.dot(h, wg_ref[:, sl], preferred_element_type=F32)
        up = jnp.dot(h, wu_ref[:, sl], preferred_element_type=F32)
        acc = acc + jnp.dot((_silu(gate) * up).astype(BF16), wd_ref[sl, :], preferred_element_type=F32)
    o_ref[...] = acc


def _dense_ffn(x, g, wg, wu, wd):
    L, d = x.shape
    f = wg.shape[1]
    tm = TM_FFN
    est = 4 * tm * d * 4 + 3 * d * f * 2 + 4 * tm * FFN_CHUNK * 4
    return pl.pallas_call(
        _ffn_kernel,
        grid=(L // tm,),
        in_specs=[pl.BlockSpec((tm, d), lambda i: (i, 0)),
                  pl.BlockSpec((1, d), lambda i: (0, 0)),
                  pl.BlockSpec((d, f), lambda i: (0, 0), pipeline_mode=pl.Buffered(1)),
                  pl.BlockSpec((d, f), lambda i: (0, 0), pipeline_mode=pl.Buffered(1)),
                  pl.BlockSpec((f, d), lambda i: (0, 0), pipeline_mode=pl.Buffered(1))],
        out_specs=pl.BlockSpec((tm, d), lambda i: (i, 0)),
        out_shape=jax.ShapeDtypeStruct((L, d), F32),
        compiler_params=pltpu.CompilerParams(
            dimension_semantics=("parallel",), vmem_limit_bytes=_vmem_limit(est)),
        name="dense_swiglu",
    )(x, g, wg, wu, wd)


def _router_kernel(x_ref, g_ref, rw_ref, ri_ref, rg_ref, cnt_ref, carry):
    i = pl.program_id(0)
    tm = x_ref.shape[0]

    @pl.when(i == 0)
    def _():
        carry[...] = jnp.zeros_like(carry)

    h = _rms(x_ref[...], g_ref[...])
    logits = jnp.dot(h, rw_ref[...], precision=lax.Precision.HIGHEST, preferred_element_type=F32)
    lane = lax.broadcasted_iota(jnp.int32, (1, LANES), 1).astype(F32)
    logits = jnp.where(lane < N_EXPERTS, logits, -jnp.inf)
    m1 = jnp.max(logits, axis=-1, keepdims=True)
    i1 = jnp.min(jnp.where(logits == m1, lane, float(LANES)), axis=-1, keepdims=True)
    rest = jnp.where(lane == i1, -jnp.inf, logits)
    m2 = jnp.max(rest, axis=-1, keepdims=True)
    i2 = jnp.min(jnp.where(rest == m2, lane, float(LANES)), axis=-1, keepdims=True)
    e21 = jnp.exp(m2 - m1)
    g1 = 1.0 / (1.0 + e21)
    g2 = e21 / (1.0 + e21)

    hot1 = lane == i1
    hot2 = lane == i2
    onehot = jnp.logical_or(hot1, hot2).astype(BF16)
    row = lax.broadcasted_iota(jnp.int32, (tm, 1), 0)
    colt = lax.broadcasted_iota(jnp.int32, (1, tm), 1)
    before = (row > colt).astype(BF16)
    seen = jnp.dot(before, onehot, preferred_element_type=F32) + carry[...]
    r1 = jnp.sum(jnp.where(hot1, seen, 0.0), axis=-1, keepdims=True)
    r2 = jnp.sum(jnp.where(hot2, seen, 0.0), axis=-1, keepdims=True)
    carry[...] = carry[...] + jnp.sum(onehot.astype(F32), axis=0, keepdims=True)

    ri = jnp.where(lane == 0, i1, jnp.where(lane == 1, i2, jnp.where(lane == 2, r1, r2)))
    ri_ref[...] = ri.astype(jnp.int32)
    rg_ref[...] = jnp.where(lane == 0, g1, g2)
    cnt_ref[...] = carry[...]


def _router(x, g, rw):
    L, d = x.shape
    tm = TM_ROUTE
    return pl.pallas_call(
        _router_kernel,
        grid=(L // tm,),
        in_specs=[pl.BlockSpec((tm, d), lambda i: (i, 0)),
                  pl.BlockSpec((1, d), lambda i: (0, 0)),
                  pl.BlockSpec((d, LANES), lambda i: (0, 0))],
        out_specs=[pl.BlockSpec((tm, LANES), lambda i: (i, 0)),
                   pl.BlockSpec((tm, LANES), lambda i: (i, 0)),
                   pl.BlockSpec((1, LANES), lambda i: (0, 0))],
        out_shape=[jax.ShapeDtypeStruct((L, LANES), jnp.int32),
                   jax.ShapeDtypeStruct((L, LANES), F32),
                   jax.ShapeDtypeStruct((1, LANES), F32)],
        scratch_shapes=[pltpu.VMEM((1, LANES), F32)],
        compiler_params=pltpu.CompilerParams(dimension_semantics=("arbitrary",)),
        name="router_top2",
    )(x, g, rw)


def _row_copy(src, dst, sem, s, d):
    return pltpu.make_async_copy(src.at[pl.ds(s, 1), :], dst.at[pl.ds(d, 1), :], sem)


def _dispatch_kernel(pos_ref, x_ref, g_ref, xs_in_ref, xs_ref, h_s, sem):
    del xs_in_ref
    tm = x_ref.shape[0]
    h_s[...] = _rms(x_ref[...], g_ref[...])

    def start(t, c):
        for k in range(TOP_K):
            _row_copy(h_s, xs_ref, sem, t, pos_ref[0, 0, TOP_K * t + k]).start()
        return c

    lax.fori_loop(0, tm, start, 0)

    def wait(t, c):
        for k in range(TOP_K):
            _row_copy(h_s, xs_ref, sem, t, pos_ref[0, 0, TOP_K * t + k]).wait()
        return c

    lax.fori_loop(0, tm, wait, 0)


def _dispatch(x, g, pos, n_sorted):
    L, d = x.shape
    tm = TM_ROUTE
    pos3 = pos.reshape(L // tm, 1, TOP_K * tm)
    xs0 = jnp.zeros((n_sorted, d), F32)
    return pl.pallas_call(
        _dispatch_kernel,
        grid=(L // tm,),
        in_specs=[pl.BlockSpec((1, 1, TOP_K * tm), lambda i: (i, 0, 0), memory_space=pltpu.SMEM),
                  pl.BlockSpec((tm, d), lambda i: (i, 0)),
                  pl.BlockSpec((1, d), lambda i: (0, 0)),
                  pl.BlockSpec(memory_space=pl.ANY)],
        out_specs=pl.BlockSpec(memory_space=pl.ANY),
        out_shape=jax.ShapeDtypeStruct((n_sorted, d), F32),
        scratch_shapes=[pltpu.VMEM((tm, d), F32), pltpu.SemaphoreType.DMA(())],
        input_output_aliases={3: 0},
        compiler_params=pltpu.CompilerParams(dimension_semantics=("arbitrary",), has_side_effects=True),
        name="moe_dispatch",
    )(pos3, x, g, xs0)


def _expert_kernel(te_ref, tv_ref, xs_ref, wg_ref, wu_ref, wd_ref, ys_ref):
    i = pl.program_id(0)

    @pl.when(tv_ref[i] == 0)
    def _():
        ys_ref[...] = jnp.zeros_like(ys_ref)

    @pl.when(tv_ref[i] != 0)
    def _():
        x = xs_ref[...].astype(BF16)
        acc = None
        for c in range(wg_ref.shape[2] // MOE_CHUNK):
            sl = slice(c * MOE_CHUNK, (c + 1) * MOE_CHUNK)
            gate = jnp.dot(x, wg_ref[0, :, sl], preferred_element_type=F32)
            up = jnp.dot(x, wu_ref[0, :, sl], preferred_element_type=F32)
            part = jnp.dot((_silu(gate) * up).astype(BF16), wd_ref[0, sl, :], preferred_element_type=F32)
            acc = part if acc is None else acc + part
        ys_ref[...] = acc


def _experts(xs, tile_expert, tile_valid, wg, wu, wd):
    n_sorted, d = xs.shape
    f = wg.shape[2]
    tm = TM_GROUP
    single = pl.Buffered(1)
    est = 3 * d * f * 2 + 4 * tm * d * 4 + 4 * tm * MOE_CHUNK * 4
    return pl.pallas_call(
        _expert_kernel,
        grid_spec=pltpu.PrefetchScalarGridSpec(
            num_scalar_prefetch=2,
            grid=(n_sorted // tm,),
            in_specs=[pl.BlockSpec((tm, d), lambda i, te, tv: (i, 0)),
                      pl.BlockSpec((1, d, f), lambda i, te, tv: (te[i], 0, 0), pipeline_mode=single),
                      pl.BlockSpec((1, d, f), lambda i, te, tv: (te[i], 0, 0), pipeline_mode=single),
                      pl.BlockSpec((1, f, d), lambda i, te, tv: (te[i], 0, 0), pipeline_mode=single)],
            out_specs=pl.BlockSpec((tm, d), lambda i, te, tv: (i, 0)),
        ),
        out_shape=jax.ShapeDtypeStruct((n_sorted, d), F32),
        compiler_params=pltpu.CompilerParams(
            dimension_semantics=("arbitrary",), vmem_limit_bytes=_vmem_limit(est)),
        name="expert_swiglu",
    )(tile_expert, tile_valid, xs, wg, wu, wd)


def _combine_kernel(pos_ref, x_ref, rg_ref, fg_ref, ys_ref, o_ref, buf, sem):
    tm = x_ref.shape[0]

    def start(t, c):
        for k in range(TOP_K):
            _row_copy(ys_ref, buf.at[k], sem, pos_ref[0, 0, TOP_K * t + k], t).start()
        return c

    lax.fori_loop(0, tm, start, 0)

    def wait(t, c):
        for k in range(TOP_K):
            _row_copy(ys_ref, buf.at[k], sem, pos_ref[0, 0, TOP_K * t + k], t).wait()
        return c

    lax.fori_loop(0, tm, wait, 0)

    rg = rg_ref[...]
    y = x_ref[...] + rg[:, 0:1] * buf[0] + rg[:, 1:2] * buf[1]
    o_ref[...] = _rms(y, fg_ref[...])


def _combine(x, rg, pos, ys, final_g):
    L, d = x.shape
    tm = TM_ROUTE
    pos3 = pos.reshape(L // tm, 1, TOP_K * tm)
    return pl.pallas_call(
        _combine_kernel,
        grid=(L // tm,),
        in_specs=[pl.BlockSpec((1, 1, TOP_K * tm), lambda i: (i, 0, 0), memory_space=pltpu.SMEM),
                  pl.BlockSpec((tm, d), lambda i: (i, 0)),
                  pl.BlockSpec((tm, LANES), lambda i: (i, 0)),
                  pl.BlockSpec((1, d), lambda i: (0, 0)),
                  pl.BlockSpec(memory_space=pl.ANY)],
        out_specs=pl.BlockSpec((tm, d), lambda i: (i, 0)),
        out_shape=jax.ShapeDtypeStruct((L, d), F32),
        scratch_shapes=[pltpu.VMEM((TOP_K, tm, d), F32), pltpu.SemaphoreType.DMA(())],
        compiler_params=pltpu.CompilerParams(dimension_semantics=("arbitrary",)),
        name="moe_combine_norm",
    )(pos3, x, rg, final_g, ys)


def _moe_ffn_final_norm(x, g2, router_w, wg, wu, wd, final_g):
    L, d = x.shape
    rw = jnp.pad(router_w, ((0, 0), (0, LANES - N_EXPERTS)))
    ri, rg, cnt = _router(x, g2, rw)
    counts = cnt[0, :N_EXPERTS].astype(jnp.int32)
    padded = ((counts + TM_GROUP - 1) // TM_GROUP) * TM_GROUP
    ends = jnp.cumsum(padded)
    starts = ends - padded
    pos = starts[ri[:, 0:TOP_K]] + ri[:, TOP_K:2 * TOP_K]
    n_sorted = TOP_K * L + N_EXPERTS * TM_GROUP
    tile_row = jnp.arange(n_sorted // TM_GROUP, dtype=jnp.int32) * TM_GROUP
    tile_valid = (tile_row < ends[-1]).astype(jnp.int32)
    last_row = jnp.minimum(tile_row, ends[-1] - TM_GROUP)
    tile_expert = jnp.sum((ends[None, :] <= last_row[:, None]).astype(jnp.int32), axis=1)
    tile_expert = jnp.minimum(tile_expert, N_EXPERTS - 1)

    xs = _dispatch(x, g2, pos, n_sorted)
    ys = _experts(xs, tile_expert, tile_valid, wg.astype(BF16), wu.astype(BF16), wd.astype(BF16))
    return _combine(x, rg, pos, ys, final_g.reshape(1, d))


def kernel(x, norm1_g, w_in, w_gate, sc_conv_w, ssm_conv_w, ssm_conv_b, ssm_dt_bias, ssm_a_log, ssm_d, ssm_norm_g, gm_ln_g, gm_ln_b, gm_ws, gm_bias, pool_map, pool_scale, w_br_a, w_br_b, w_br_c, w_br_d, w_out, norm2_g, ffn_wg, ffn_wu, ffn_wd, router_w, moe_wg, moe_wu, moe_wd, final_g):
    bsz, L, d = x.shape
    assert bsz == 1 and d == D_MODEL and L % TM_FFN == 0
    depth = norm1_g.shape[0]
    assert depth == 2, "layer 0 uses the dense SwiGLU, layer 1 the expert SwiGLU followed by the final norm"
    xt = x.reshape(L, d)
    rexp = _head_expand_matrix()
    pad_h = LANES - SSM_HEADS
    for layer in range(depth):
        w_in_l = w_in[layer]
        p = {
            "norm1": norm1_g[layer].reshape(1, d),
            "w_in": jnp.concatenate(
                [w_in_l[:, :IN_DT_END], jnp.zeros((d, pad_h), F32), w_in_l[:, IN_DT_END:]], axis=1).astype(BF16),
            "w_gate": w_gate[layer].astype(BF16),
            "scw": sc_conv_w[layer],
            "mcw": ssm_conv_w[layer],
            "mcb": ssm_conv_b[layer].reshape(1, -1),
            "dtb": jnp.pad(ssm_dt_bias[layer], (0, pad_h)).reshape(1, LANES),
            "alog": jnp.pad(ssm_a_log[layer], (0, pad_h)).reshape(1, LANES),
            "dskip": jnp.repeat(ssm_d[layer], SSM_HEAD_DIM).reshape(1, SSM_D_INNER),
            "ng": ssm_norm_g[layer].reshape(1, -1),
            "lng": gm_ln_g[layer].reshape(1, -1),
            "lnb": gm_ln_b[layer].reshape(1, -1),
            "gws": gm_ws[layer],
            "gbias": jnp.repeat(gm_bias[layer].T, GM_WIDTH // GM_GROUPS, axis=1),
            "pmap": pool_map[layer].astype(BF16),
            "pscale": pool_scale[layer].reshape(1, -1),
            "rexp": rexp,
            "wbr": jnp.concatenate([w_br_a[layer], w_br_b[layer], w_br_c[layer], w_br_d[layer]], axis=0).astype(BF16),
            "wout": w_out[layer].astype(BF16),
        }
        xt = _mixer(xt, p)
        idx = layer // 2
        g2 = norm2_g[layer].reshape(1, d)
        if layer % 2 == 0:
            xt = _dense_ffn(xt, g2, ffn_wg[idx].astype(BF16), ffn_wu[idx].astype(BF16), ffn_wd[idx].astype(BF16))
        else:
            xt = _moe_ffn_final_norm(xt, g2, router_w[idx], moe_wg[idx], moe_wu[idx], moe_wd[idx], final_g)
    return xt.reshape(bsz, L, d)
```

```python
import numpy as np
import jax
import jax.numpy as jnp
from jax import lax
from jax.experimental import pallas as pl
from jax.experimental.pallas import tpu as pltpu

F32 = jnp.float32
BF16 = jnp.bfloat16
EPS = 1e-6

LANES = 128
SUBLANES = 8

D_MODEL = 1024
SC_WIDTH = 512
SSM_D_INNER = 1024
SSM_HEAD_DIM = 64
SSM_HEADS = 16
SSM_GROUPS = 4
SSM_STATE = 128
SSM_CONV_DIM = 2048
GM_WIDTH = 512
GM_GROUPS = 4
GM_BLOCK = 128
POOL_WIDTH = 512
POOL_WINDOWS = (2, 4, 8, 16)
POOL_HALO = 16
N_BRANCH = 4
N_EXPERTS = 8
TOP_K = 2

P_AH, P_AB, P_AC = 0, 512, 1024
P_Z = 1536
P_XBC = 2560
P_DT = 4608
P_U = P_DT + LANES
P_V = P_U + GM_WIDTH
P_PD = P_V + GM_WIDTH
P_TOTAL = P_PD + POOL_WIDTH
IN_DT_END = 4624

BR_TOTAL = SC_WIDTH + SSM_D_INNER + GM_WIDTH + POOL_WIDTH

TM_MIX = 256
PROJ_CHUNK = 1792
TM_FFN = 512
TM_ROUTER = 512
TM_ROUTE = 256
TM_GROUP = 256
FFN_CHUNK = 1408
MOE_CHUNK = 512
DMA_UNROLL = 8


def _vmem_limit(nbytes):
    return int(min(nbytes + (8 << 20), 60 << 20))


def _rms(x, g):
    return x * lax.rsqrt(jnp.mean(x * x, axis=-1, keepdims=True) + EPS) * g


def _silu(x):
    return x * jax.nn.sigmoid(x)


def _shift_rows(ext, k, halo, tm):
    return pltpu.roll(ext, k, 0)[halo:halo + tm]


def _split3(x):
    hi = x.astype(BF16)
    r1 = x - hi.astype(F32)
    mid = r1.astype(BF16)
    lo = (r1 - mid.astype(F32)).astype(BF16)
    return hi, mid, lo


def _mixer_kernel(x_ref, n1_ref, win_ref, wgate_ref, scw_ref, mcw_ref, mcb_ref, dtb_ref, alog_ref,
                  dskip_ref, ng_ref, lng_ref, lnb_ref, gws_ref, gbias_ref, pmap_ref, pscale_ref,
                  rexp_ref, wbr_ref, wout_ref, o_ref,
                  halo_a, halo_x, halo_p, state, h_s, proj_ref, xbc_s, y_s):
    i = pl.program_id(0)
    tm = x_ref.shape[0]

    @pl.when(i == 0)
    def _init():
        halo_a[...] = jnp.zeros_like(halo_a)
        halo_x[...] = jnp.zeros_like(halo_x)
        halo_p[...] = jnp.zeros_like(halo_p)
        state[...] = jnp.zeros_like(state)

    h_s[...] = _rms(x_ref[...], n1_ref[...]).astype(BF16)
    for c0 in range(0, P_TOTAL, PROJ_CHUNK):
        sl = slice(c0, min(c0 + PROJ_CHUNK, P_TOTAL))
        proj_ref[:, sl] = jnp.dot(h_s[...], win_ref[:, sl], preferred_element_type=F32)

    row = lax.broadcasted_iota(jnp.int32, (tm, 1), 0)
    lane = lax.broadcasted_iota(jnp.int32, (1, LANES), 1)

    ch = proj_ref[:, P_AC:P_AC + SC_WIDTH] * proj_ref[:, P_AH:P_AH + SC_WIDTH]
    ext = jnp.concatenate([halo_a[...], ch], axis=0)
    scw = scw_ref[...]
    conv_a = (ch * scw[2:3, :]
              + _shift_rows(ext, 1, SUBLANES, tm) * scw[1:2, :]
              + _shift_rows(ext, 2, SUBLANES, tm) * scw[0:1, :])
    halo_a[...] = ch[tm - SUBLANES:tm, :]
    y_s[:, 0:SC_WIDTH] = (proj_ref[:, P_AB:P_AB + SC_WIDTH] * conv_a).astype(BF16)

    cw = 512
    for c in range(SSM_CONV_DIM // cw):
        xc = proj_ref[:, P_XBC + c * cw:P_XBC + (c + 1) * cw]
        extx = jnp.concatenate([halo_x[:, c * cw:(c + 1) * cw], xc], axis=0)
        mcw = mcw_ref[:, c * cw:(c + 1) * cw]
        conv = (xc * mcw[3:4, :]
                + _shift_rows(extx, 1, SUBLANES, tm) * mcw[2:3, :]
                + _shift_rows(extx, 2, SUBLANES, tm) * mcw[1:2, :]
                + _shift_rows(extx, 3, SUBLANES, tm) * mcw[0:1, :]
                + mcb_ref[:, c * cw:(c + 1) * cw])
        halo_x[:, c * cw:(c + 1) * cw] = xc[tm - SUBLANES:tm, :]
        xbc_s[:, c * cw:(c + 1) * cw] = _silu(conv)

    dt_in = proj_ref[:, P_DT:P_DT + LANES] + dtb_ref[...]
    dt = jnp.maximum(dt_in, 0.0) + jnp.log1p(jnp.exp(-jnp.abs(dt_in)))
    da = dt * (-jnp.exp(alog_ref[...]))
    col = lax.broadcasted_iota(jnp.int32, (1, tm), 1)
    causal = row >= col
    head_lane = lane < SSM_HEADS
    d_hi, d_mid, d_lo = _split3(jnp.where(head_lane, da, 0.0))
    d3 = (d_hi.astype(F32) + pltpu.roll(d_mid.astype(F32), SSM_HEADS, 1)
          + pltpu.roll(d_lo.astype(F32), 2 * SSM_HEADS, 1)).astype(BF16)
    c3 = jnp.dot(causal.astype(BF16), d3, preferred_element_type=F32)
    cs = c3 + pltpu.roll(c3, LANES - SSM_HEADS, 1) + pltpu.roll(c3, LANES - 2 * SSM_HEADS, 1)
    cs = jnp.where(head_lane, cs, 0.0)
    decay_in = jnp.exp(cs)
    to_end = jnp.exp(cs[tm - 1:tm, :] - cs) * dt
    packed = (jnp.where(head_lane, decay_in, 0.0)
              + pltpu.roll(jnp.where(head_lane, to_end, 0.0), SSM_HEADS, 1))
    hi, mid, lo = _split3(packed)
    x3 = (hi.astype(F32) + pltpu.roll(mid.astype(F32), 2 * SSM_HEADS, 1)
          + pltpu.roll(lo.astype(F32), 4 * SSM_HEADS, 1)).astype(BF16)
    expanded = jnp.dot(x3, rexp_ref[...], preferred_element_type=F32)
    decay_cols = expanded[:, 0:SSM_D_INNER]
    toend_cols = expanded[:, SSM_D_INNER:2 * SSM_D_INNER]

    tpk = jnp.where(head_lane, cs, 0.0) + pltpu.roll(jnp.where(head_lane, dt, 0.0), SSM_HEADS, 1)
    tpk_t = tpk.T

    gw = SSM_D_INNER // SSM_GROUPS
    lo_half = lane < SSM_HEAD_DIM
    for g in range(SSM_GROUPS):
        b_g = xbc_s[:, SSM_D_INNER + g * SSM_STATE:SSM_D_INNER + (g + 1) * SSM_STATE].astype(BF16)
        c_off = SSM_D_INNER + SSM_GROUPS * SSM_STATE
        c_g = xbc_s[:, c_off + g * SSM_STATE:c_off + (g + 1) * SSM_STATE].astype(BF16)
        cb = lax.dot_general(c_g, b_g, (((1,), (1,)), ((), ())), preferred_element_type=F32)
        s_g = state[g * SSM_STATE:(g + 1) * SSM_STATE, :]
        y_off = jnp.dot(c_g, s_g.astype(BF16), preferred_element_type=F32) * decay_cols[:, g * gw:(g + 1) * gw]
        xs_g = xbc_s[:, g * gw:(g + 1) * gw]
        pieces = []
        for jj in range(2):
            xs_t = xs_g[:, jj * LANES:(jj + 1) * LANES].astype(BF16)
            acc = None
            for kk in range(2):
                h = g * 4 + jj * 2 + kk
                seg = jnp.exp(jnp.where(causal, cs[:, h:h + 1] - tpk_t[h:h + 1, :], -jnp.inf))
                scores = (cb * seg * tpk_t[SSM_HEADS + h:SSM_HEADS + h + 1, :]).astype(BF16)
                keep = lo_half if kk == 0 else jnp.logical_not(lo_half)
                part = jnp.dot(scores, jnp.where(keep, xs_t, jnp.zeros_like(xs_t)),
                               preferred_element_type=F32)
                acc = part if acc is None else acc + part
            pieces.append(acc)
        y_g = jnp.concatenate(pieces, axis=1) + y_off + xs_g * dskip_ref[:, g * gw:(g + 1) * gw]
        upd = jnp.dot(b_g.T, (toend_cols[:, g * gw:(g + 1) * gw] * xs_g).astype(BF16),
                      preferred_element_type=F32)
        state[g * SSM_STATE:(g + 1) * SSM_STATE, :] = decay_cols[tm - 1:tm, g * gw:(g + 1) * gw] * s_g + upd
        y_g = y_g * _silu(proj_ref[:, P_Z + g * gw:P_Z + (g + 1) * gw])
        y_g = y_g * lax.rsqrt(jnp.mean(y_g * y_g, axis=-1, keepdims=True) + EPS)
        y_s[:, SC_WIDTH + g * gw:SC_WIDTH + (g + 1) * gw] = (y_g * ng_ref[:, g * gw:(g + 1) * gw]).astype(BF16)

    def gelu(t):
        return 0.5 * t * (1.0 + jnp.tanh(0.7978845608028654 * (t + 0.044715 * (t * t * t))))

    v = gelu(proj_ref[:, P_V:P_V + GM_WIDTH])
    mu = jnp.mean(v, axis=-1, keepdims=True)
    vc = v - mu
    var = jnp.mean(vc * vc, axis=-1, keepdims=True)
    vf = (vc * lax.rsqrt(var + EPS) * lng_ref[...] + lnb_ref[...]).astype(BF16)
    r_b = lax.broadcasted_iota(jnp.int32, (GM_BLOCK, 1), 0)
    c_b = lax.broadcasted_iota(jnp.int32, (1, GM_BLOCK), 1)
    gc = GM_WIDTH // GM_GROUPS
    for g in range(GM_GROUPS):
        wsm = jnp.where(r_b >= c_b, gws_ref[g], 0.0).astype(BF16)
        blocks = [jnp.dot(wsm, vf[b * GM_BLOCK:(b + 1) * GM_BLOCK, g * gc:(g + 1) * gc],
                          preferred_element_type=F32) + gbias_ref[:, g * gc:(g + 1) * gc]
                  for b in range(tm // GM_BLOCK)]
        s_sp = jnp.concatenate(blocks, axis=0)
        u_g = gelu(proj_ref[:, P_U + g * gc:P_U + (g + 1) * gc])
        off = SC_WIDTH + SSM_D_INNER
        y_s[:, off + g * gc:off + (g + 1) * gc] = (u_g * s_sp).astype(BF16)

    t_glob = i * tm + row
    pc = POOL_WIDTH // len(POOL_WINDOWS)
    for g, w in enumerate(POOL_WINDOWS):
        pd_g = proj_ref[:, P_PD + g * pc:P_PD + (g + 1) * pc]
        s = jnp.concatenate([halo_p[:, g * pc:(g + 1) * pc], pd_g], axis=0)
        k = 1
        while k < w:
            s = s + pltpu.roll(s, k, 0)
            k *= 2
        win = s[POOL_HALO:POOL_HALO + tm]
        inv_cnt = 1.0 / jnp.minimum(t_glob + 1, w).astype(F32)
        pooled = (win * inv_cnt - pd_g).astype(BF16)
        halo_p[:, g * pc:(g + 1) * pc] = pd_g[tm - POOL_HALO:tm, :]
        off = SC_WIDTH + SSM_D_INNER + GM_WIDTH
        y_d = jnp.dot(pooled, pmap_ref[g], preferred_element_type=F32) * pscale_ref[:, g * pc:(g + 1) * pc]
        y_s[:, off + g * pc:off + (g + 1) * pc] = y_d.astype(BF16)

    bounds = (0, SC_WIDTH, SC_WIDTH + SSM_D_INNER, SC_WIDTH + SSM_D_INNER + GM_WIDTH, BR_TOTAL)
    merged = None
    for b in range(N_BRANCH):
        br = jnp.dot(y_s[:, bounds[b]:bounds[b + 1]], wbr_ref[bounds[b]:bounds[b + 1], :],
                     preferred_element_type=F32)
        gate = jax.nn.sigmoid(jnp.dot(h_s[...], wgate_ref[:, b * D_MODEL:(b + 1) * D_MODEL],
                                      preferred_element_type=F32))
        term = gate * br
        merged = term if merged is None else merged + term
    o_ref[...] = x_ref[...] + jnp.dot(merged.astype(BF16), wout_ref[...], preferred_element_type=F32)


def _head_expand_matrix():
    r = np.zeros((LANES, 2 * SSM_D_INNER), np.float32)
    for piece in range(3):
        for q in range(2):
            for h in range(SSM_HEADS):
                rr = piece * 2 * SSM_HEADS + q * SSM_HEADS + h
                r[rr, q * SSM_D_INNER + h * SSM_HEAD_DIM:q * SSM_D_INNER + (h + 1) * SSM_HEAD_DIM] = 1.0
    return jnp.asarray(r, BF16)


def _mixer(x, p):
    L = x.shape[0]
    tm = TM_MIX
    full = lambda a: pl.BlockSpec(a.shape, lambda i, _n=a.ndim: (0,) * _n, pipeline_mode=pl.Buffered(1))
    consts = [p["norm1"], p["w_in"], p["w_gate"], p["scw"], p["mcw"], p["mcb"], p["dtb"], p["alog"],
              p["dskip"], p["ng"], p["lng"], p["lnb"], p["gws"], p["gbias"], p["pmap"], p["pscale"],
              p["rexp"], p["wbr"], p["wout"]]
    est = (4 * tm * D_MODEL * 4 + sum(int(a.size) * a.dtype.itemsize for a in consts)
           + tm * (D_MODEL * 2 + P_TOTAL * 4 + SSM_CONV_DIM * 4 + BR_TOTAL * 2) + 8 * tm * D_MODEL * 4)
    return pl.pallas_call(
        _mixer_kernel,
        grid=(L // tm,),
        in_specs=[pl.BlockSpec((tm, D_MODEL), lambda i: (i, 0))] + [full(a) for a in consts],
        out_specs=pl.BlockSpec((tm, D_MODEL), lambda i: (i, 0)),
        out_shape=jax.ShapeDtypeStruct((L, D_MODEL), F32),
        scratch_shapes=[
            pltpu.VMEM((SUBLANES, SC_WIDTH), F32),
            pltpu.VMEM((SUBLANES, SSM_CONV_DIM), F32),
            pltpu.VMEM((POOL_HALO, POOL_WIDTH), F32),
            pltpu.VMEM((SSM_GROUPS * SSM_STATE, SSM_D_INNER // SSM_GROUPS), F32),
            pltpu.VMEM((tm, D_MODEL), BF16),
            pltpu.VMEM((tm, P_TOTAL), F32),
            pltpu.VMEM((tm, SSM_CONV_DIM), F32),
            pltpu.VMEM((tm, BR_TOTAL), BF16),
        ],
        compiler_params=pltpu.CompilerParams(
            dimension_semantics=("arbitrary",), vmem_limit_bytes=_vmem_limit(est)),
        name="mixers",
    )(x, *consts)


def _ffn_kernel(x_ref, g_ref, wg_ref, wu_ref, wd_ref, o_ref):
    x = x_ref[...]
    h = _rms(x, g_ref[...]).astype(BF16)
    acc = x
    for c in range(wg_ref.shape[1] // FFN_CHUNK):
        sl = slice(c * FFN_CHUNK, (c + 1) * FFN_CHUNK)
        gate = jnp.dot(h, wg_ref[:, sl], preferred_element_type=F32)
        up = jnp.dot(h, wu_ref[:, sl], preferred_element_type=F32)
        acc = acc + jnp.dot((_silu(gate) * up).astype(BF16), wd_ref[sl, :], preferred_element_type=F32)
    o_ref[...] = acc


def _dense_ffn(x, g, wg, wu, wd):
    L, d = x.shape
    f = wg.shape[1]
    tm = TM_FFN
    est = 4 * tm * d * 4 + 3 * d * f * 2 + 4 * tm * FFN_CHUNK * 4
    return pl.pallas_call(
        _ffn_kernel,
        grid=(L // tm,),
        in_specs=[pl.BlockSpec((tm, d), lambda i: (i, 0)),
                  pl.BlockSpec((1, d), lambda i: (0, 0)),
                  pl.BlockSpec((d, f), lambda i: (0, 0), pipeline_mode=pl.Buffered(1)),
                  pl.BlockSpec((d, f), lambda i: (0, 0), pipeline_mode=pl.Buffered(1)),
                  pl.BlockSpec((f, d), lambda i: (0, 0), pipeline_mode=pl.Buffered(1))],
        out_specs=pl.BlockSpec((tm, d), lambda i: (i, 0)),
        out_shape=jax.ShapeDtypeStruct((L, d), F32),
        compiler_params=pltpu.CompilerParams(
            dimension_semantics=("parallel",), vmem_limit_bytes=_vmem_limit(est)),
        name="dense_swiglu",
    )(x, g, wg, wu, wd)


def _router_kernel(x_ref, g_ref, rw_ref, ri_ref, rg_ref, cnt_ref, carry):
    i = pl.program_id(0)
    tm = x_ref.shape[0]

    @pl.when(i == 0)
    def _():
        carry[...] = jnp.zeros_like(carry)

    h = _rms(x_ref[...], g_ref[...])
    pieces = jnp.dot(jnp.concatenate(_split3(h), axis=0), rw_ref[...], preferred_element_type=F32)
    s3 = pieces[0:tm] + pieces[tm:2 * tm] + pieces[2 * tm:3 * tm]
    logits = s3 + pltpu.roll(s3, LANES - N_EXPERTS, 1) + pltpu.roll(s3, LANES - 2 * N_EXPERTS, 1)
    lane = lax.broadcasted_iota(jnp.int32, (1, LANES), 1).astype(F32)
    logits = jnp.where(lane < N_EXPERTS, logits, -jnp.inf)
    m1 = jnp.max(logits, axis=-1, keepdims=True)
    i1 = jnp.min(jnp.where(logits == m1, lane, float(LANES)), axis=-1, keepdims=True)
    rest = jnp.where(lane == i1, -jnp.inf, logits)
    m2 = jnp.max(rest, axis=-1, keepdims=True)
    i2 = jnp.min(jnp.where(rest == m2, lane, float(LANES)), axis=-1, keepdims=True)
    e21 = jnp.exp(m2 - m1)
    g1 = 1.0 / (1.0 + e21)
    g2 = e21 / (1.0 + e21)

    hot1 = lane == i1
    hot2 = lane == i2
    onehot = jnp.logical_or(hot1, hot2).astype(BF16)
    row = lax.broadcasted_iota(jnp.int32, (tm, 1), 0)
    colt = lax.broadcasted_iota(jnp.int32, (1, tm), 1)
    before = (row > colt).astype(BF16)
    seen = jnp.dot(before, onehot, preferred_element_type=F32) + carry[...]
    r1 = jnp.sum(jnp.where(hot1, seen, 0.0), axis=-1, keepdims=True)
    r2 = jnp.sum(jnp.where(hot2, seen, 0.0), axis=-1, keepdims=True)
    carry[...] = carry[...] + jnp.sum(onehot.astype(F32), axis=0, keepdims=True)

    ri = jnp.where(lane == 0, i1, jnp.where(lane == 1, i2, jnp.where(lane == 2, r1, r2)))
    ri_ref[...] = ri.astype(jnp.int32)
    rg_ref[...] = jnp.where(lane == 0, g1, g2)
    cnt_ref[...] = carry[...]


def _router(x, g, rw):
    L, d = x.shape
    tm = TM_ROUTER
    return pl.pallas_call(
        _router_kernel,
        grid=(L // tm,),
        in_specs=[pl.BlockSpec((tm, d), lambda i: (i, 0)),
                  pl.BlockSpec((1, d), lambda i: (0, 0)),
                  pl.BlockSpec((d, LANES), lambda i: (0, 0))],
        out_specs=[pl.BlockSpec((tm, LANES), lambda i: (i, 0)),
                   pl.BlockSpec((tm, LANES), lambda i: (i, 0)),
                   pl.BlockSpec((1, LANES), lambda i: (0, 0))],
        out_shape=[jax.ShapeDtypeStruct((L, LANES), jnp.int32),
                   jax.ShapeDtypeStruct((L, LANES), F32),
                   jax.ShapeDtypeStruct((1, LANES), F32)],
        scratch_shapes=[pltpu.VMEM((1, LANES), F32)],
        compiler_params=pltpu.CompilerParams(dimension_semantics=("arbitrary",)),
        name="router_top2",
    )(x, g, rw)


def _row_copy(src, dst, sem, s, d):
    return pltpu.make_async_copy(src.at[pl.ds(s, 1), :], dst.at[pl.ds(d, 1), :], sem)


def _start_then_wait_row_copies(tm, copy_of):
    def start(t, c):
        for k in range(TOP_K):
            copy_of(t, k).start(priority=k)
        return c

    def wait(t, c):
        for k in range(TOP_K):
            copy_of(t, k).wait()
        return c

    lax.fori_loop(0, tm, start, 0, unroll=DMA_UNROLL)
    lax.fori_loop(0, tm, wait, 0, unroll=DMA_UNROLL)


def _dispatch_kernel(pos_ref, x_ref, g_ref, xs_in_ref, xs_ref, h_s, sem):
    del xs_in_ref
    tm = x_ref.shape[0]
    h_s[...] = _rms(x_ref[...], g_ref[...])
    _start_then_wait_row_copies(
        tm, lambda t, k: _row_copy(h_s, xs_ref, sem, t, pos_ref[0, 0, TOP_K * t + k]))


def _dispatch(x, g, pos, n_sorted):
    L, d = x.shape
    tm = TM_ROUTE
    pos3 = pos.reshape(L // tm, 1, TOP_K * tm)
    xs0 = jnp.zeros((n_sorted, d), F32)
    return pl.pallas_call(
        _dispatch_kernel,
        grid=(L // tm,),
        in_specs=[pl.BlockSpec((1, 1, TOP_K * tm), lambda i: (i, 0, 0), memory_space=pltpu.SMEM),
                  pl.BlockSpec((tm, d), lambda i: (i, 0)),
                  pl.BlockSpec((1, d), lambda i: (0, 0)),
                  pl.BlockSpec(memory_space=pl.ANY)],
        out_specs=pl.BlockSpec(memory_space=pl.ANY),
        out_shape=jax.ShapeDtypeStruct((n_sorted, d), F32),
        scratch_shapes=[pltpu.VMEM((tm, d), F32), pltpu.SemaphoreType.DMA(())],
        input_output_aliases={3: 0},
        compiler_params=pltpu.CompilerParams(dimension_semantics=("arbitrary",), has_side_effects=True),
        name="moe_dispatch",
    )(pos3, x, g, xs0)


def _expert_kernel(te_ref, tv_ref, xs_ref, wg_ref, wu_ref, wd_ref, ys_ref):
    i = pl.program_id(0)

    @pl.when(tv_ref[i] == 0)
    def _():
        ys_ref[...] = jnp.zeros_like(ys_ref)

    @pl.when(tv_ref[i] != 0)
    def _():
        x = xs_ref[...].astype(BF16)
        acc = None
        for c in range(wg_ref.shape[2] // MOE_CHUNK):
            sl = slice(c * MOE_CHUNK, (c + 1) * MOE_CHUNK)
            gate = jnp.dot(x, wg_ref[0, :, sl], preferred_element_type=F32)
            up = jnp.dot(x, wu_ref[0, :, sl], preferred_element_type=F32)
            part = jnp.dot((_silu(gate) * up).astype(BF16), wd_ref[0, sl, :], preferred_element_type=F32)
            acc = part if acc is None else acc + part
        ys_ref[...] = acc


def _experts(xs, tile_expert, tile_valid, wg, wu, wd):
    n_sorted, d = xs.shape
    f = wg.shape[2]
    tm = TM_GROUP
    single = pl.Buffered(1)
    est = 3 * d * f * 2 + 4 * tm * d * 4 + 4 * tm * MOE_CHUNK * 4
    return pl.pallas_call(
        _expert_kernel,
        grid_spec=pltpu.PrefetchScalarGridSpec(
            num_scalar_prefetch=2,
            grid=(n_sorted // tm,),
            in_specs=[pl.BlockSpec((tm, d), lambda i, te, tv: (i, 0)),
                      pl.BlockSpec((1, d, f), lambda i, te, tv: (te[i], 0, 0), pipeline_mode=single),
                      pl.BlockSpec((1, d, f), lambda i, te, tv: (te[i], 0, 0), pipeline_mode=single),
                      pl.BlockSpec((1, f, d), lambda i, te, tv: (te[i], 0, 0), pipeline_mode=single)],
            out_specs=pl.BlockSpec((tm, d), lambda i, te, tv: (i, 0)),
        ),
        out_shape=jax.ShapeDtypeStruct((n_sorted, d), F32),
        compiler_params=pltpu.CompilerParams(
            dimension_semantics=("arbitrary",), vmem_limit_bytes=_vmem_limit(est)),
        name="expert_swiglu",
    )(tile_expert, tile_valid, xs, wg, wu, wd)


def _combine_kernel(pos_ref, x_ref, rg_ref, fg_ref, ys_ref, o_ref, buf, sem):
    tm = x_ref.shape[0]
    _start_then_wait_row_copies(
        tm, lambda t, k: _row_copy(ys_ref, buf.at[k], sem, pos_ref[0, 0, TOP_K * t + k], t))
    rg = rg_ref[...]
    y = x_ref[...] + rg[:, 0:1] * buf[0] + rg[:, 1:2] * buf[1]
    o_ref[...] = _rms(y, fg_ref[...])


def _combine(x, rg, pos, ys, final_g):
    L, d = x.shape
    tm = TM_ROUTE
    pos3 = pos.reshape(L // tm, 1, TOP_K * tm)
    return pl.pallas_call(
        _combine_kernel,
        grid=(L // tm,),
        in_specs=[pl.BlockSpec((1, 1, TOP_K * tm), lambda i: (i, 0, 0), memory_space=pltpu.SMEM),
                  pl.BlockSpec((tm, d), lambda i: (i, 0)),
                  pl.BlockSpec((tm, LANES), lambda i: (i, 0)),
                  pl.BlockSpec((1, d), lambda i: (0, 0)),
                  pl.BlockSpec(memory_space=pl.ANY)],
        out_specs=pl.BlockSpec((tm, d), lambda i: (i, 0)),
        out_shape=jax.ShapeDtypeStruct((L, d), F32),
        scratch_shapes=[pltpu.VMEM((TOP_K, tm, d), F32), pltpu.SemaphoreType.DMA(())],
        compiler_params=pltpu.CompilerParams(dimension_semantics=("arbitrary",)),
        name="moe_combine_norm",
    )(pos3, x, rg, final_g, ys)


def _moe_ffn_final_norm(x, g2, router_w, wg, wu, wd, final_g):
    L, d = x.shape
    rw_hi = router_w.astype(BF16)
    rw_mid = (router_w - rw_hi.astype(F32)).astype(BF16)
    rw_lo = (router_w - rw_hi.astype(F32) - rw_mid.astype(F32)).astype(BF16)
    rw = jnp.pad(jnp.concatenate([rw_hi, rw_mid, rw_lo], axis=1), ((0, 0), (0, LANES - 3 * N_EXPERTS)))
    ri, rg, cnt = _router(x, g2, rw)
    counts = cnt[0, :N_EXPERTS].astype(jnp.int32)
    padded = ((counts + TM_GROUP - 1) // TM_GROUP) * TM_GROUP
    ends = jnp.cumsum(padded)
    starts = ends - padded
    pos = starts[ri[:, 0:TOP_K]] + ri[:, TOP_K:2 * TOP_K]
    n_sorted = TOP_K * L + N_EXPERTS * TM_GROUP
    tile_row = jnp.arange(n_sorted // TM_GROUP, dtype=jnp.int32) * TM_GROUP
    tile_valid = (tile_row < ends[-1]).astype(jnp.int32)
    last_row = jnp.minimum(tile_row, ends[-1] - TM_GROUP)
    tile_expert = jnp.sum((ends[None, :] <= last_row[:, None]).astype(jnp.int32), axis=1)
    tile_expert = jnp.minimum(tile_expert, N_EXPERTS - 1)

    xs = _dispatch(x, g2, pos, n_sorted)
    ys = _experts(xs, tile_expert, tile_valid, wg.astype(BF16), wu.astype(BF16), wd.astype(BF16))
    return _combine(x, rg, pos, ys, final_g.reshape(1, d))


def kernel(x, norm1_g, w_in, w_gate, sc_conv_w, ssm_conv_w, ssm_conv_b, ssm_dt_bias, ssm_a_log, ssm_d, ssm_norm_g, gm_ln_g, gm_ln_b, gm_ws, gm_bias, pool_map, pool_scale, w_br_a, w_br_b, w_br_c, w_br_d, w_out, norm2_g, ffn_wg, ffn_wu, ffn_wd, router_w, moe_wg, moe_wu, moe_wd, final_g):
    bsz, L, d = x.shape
    assert bsz == 1 and d == D_MODEL and L % TM_FFN == 0
    depth = norm1_g.shape[0]
    assert depth == 2, "layer 0 uses the dense SwiGLU, layer 1 the expert SwiGLU followed by the final norm"
    xt = x.reshape(L, d)
    rexp = _head_expand_matrix()
    pad_h = LANES - SSM_HEADS
    for layer in range(depth):
        w_in_l = w_in[layer]
        p = {
            "norm1": norm1_g[layer].reshape(1, d),
            "w_in": jnp.concatenate(
                [w_in_l[:, :IN_DT_END], jnp.zeros((d, pad_h), F32), w_in_l[:, IN_DT_END:]], axis=1).astype(BF16),
            "w_gate": w_gate[layer].astype(BF16),
            "scw": sc_conv_w[layer],
            "mcw": ssm_conv_w[layer],
            "mcb": ssm_conv_b[layer].reshape(1, -1),
            "dtb": jnp.pad(ssm_dt_bias[layer], (0, pad_h)).reshape(1, LANES),
            "alog": jnp.pad(ssm_a_log[layer], (0, pad_h)).reshape(1, LANES),
            "dskip": jnp.repeat(ssm_d[layer], SSM_HEAD_DIM).reshape(1, SSM_D_INNER),
            "ng": ssm_norm_g[layer].reshape(1, -1),
            "lng": gm_ln_g[layer].reshape(1, -1),
            "lnb": gm_ln_b[layer].reshape(1, -1),
            "gws": gm_ws[layer],
            "gbias": jnp.repeat(gm_bias[layer].T, GM_WIDTH // GM_GROUPS, axis=1),
            "pmap": pool_map[layer].astype(BF16),
            "pscale": pool_scale[layer].reshape(1, -1),
            "rexp": rexp,
            "wbr": jnp.concatenate([w_br_a[layer], w_br_b[layer], w_br_c[layer], w_br_d[layer]], axis=0).astype(BF16),
            "wout": w_out[layer].astype(BF16),
        }
        xt = _mixer(xt, p)
        idx = layer // 2
        g2 = norm2_g[layer].reshape(1, d)
        if layer % 2 == 0:
            xt = _dense_ffn(xt, g2, ffn_wg[idx].astype(BF16), ffn_wu[idx].astype(BF16), ffn_wd[idx].astype(BF16))
        else:
            xt = _moe_ffn_final_norm(xt, g2, router_w[idx], moe_wg[idx], moe_wu[idx], moe_wd[idx], final_g)
    return xt.reshape(bsz, L, d)
```

```python
import numpy as np
import jax
import jax.numpy as jnp
from jax import lax
from jax.experimental import pallas as pl
from jax.experimental.pallas import tpu as pltpu

F32 = jnp.float32
BF16 = jnp.bfloat16
EPS = 1e-6

LANES = 128
SUBLANES = 8

D_MODEL = 1024
SC_WIDTH = 512
SSM_D_INNER = 1024
SSM_HEAD_DIM = 64
SSM_HEADS = 16
SSM_GROUPS = 4
SSM_STATE = 128
SSM_CONV_DIM = 2048
GM_WIDTH = 512
GM_GROUPS = 4
GM_BLOCK = 128
POOL_WIDTH = 512
POOL_WINDOWS = (2, 4, 8, 16)
POOL_HALO = 16
N_BRANCH = 4
N_EXPERTS = 8
TOP_K = 2

P_AH, P_AB, P_AC = 0, 512, 1024
P_Z = 1536
P_XBC = 2560
P_DT = 4608
P_U = P_DT + LANES
P_V = P_U + GM_WIDTH
P_PD = P_V + GM_WIDTH
P_TOTAL = P_PD + POOL_WIDTH
IN_DT_END = 4624

BR_TOTAL = SC_WIDTH + SSM_D_INNER + GM_WIDTH + POOL_WIDTH

TM_MIX = 256
SSD_CHUNK = 256
PROJ_CHUNK = 1792
TM_FFN = 512
TM_ROUTER = 512
TM_ROUTE = 256
TM_GROUP = 512
FFN_CHUNK = 1408
MOE_CHUNK = 512
DMA_UNROLL = 8


def _vmem_limit(nbytes):
    return int(min(nbytes + (8 << 20), 60 << 20))


def _rms(x, g):
    return x * lax.rsqrt(jnp.mean(x * x, axis=-1, keepdims=True) + EPS) * g


def _silu(x):
    return x * jax.nn.sigmoid(x)


def _shift_rows(ext, k, halo, tm):
    return pltpu.roll(ext, k, 0)[halo:halo + tm]


def _split3(x):
    hi = x.astype(BF16)
    r1 = x - hi.astype(F32)
    mid = r1.astype(BF16)
    lo = (r1 - mid.astype(F32)).astype(BF16)
    return hi, mid, lo


def _mixer_kernel(x_ref, n1_ref, win_ref, windt_ref, wintail_ref, wgate_ref, scw_ref, mcw_ref, mcb_ref,
                  dtb_ref, alog_ref, dskip_ref, ng_ref, lng_ref, lnb_ref, gws_ref, gbias_ref, pmap_ref,
                  pscale_ref, rexp_ref, wbra_ref, wbrb_ref, wbrc_ref, wbrd_ref, wout_ref, o_ref,
                  halo_a, halo_x, halo_p, state, h_s, proj_ref, dt_s, xbc_s, y_s):
    i = pl.program_id(0)
    tm = x_ref.shape[0]

    @pl.when(i == 0)
    def _init():
        halo_a[...] = jnp.zeros_like(halo_a)
        halo_x[...] = jnp.zeros_like(halo_x)
        halo_p[...] = jnp.zeros_like(halo_p)
        state[...] = jnp.zeros_like(state)

    h_s[...] = _rms(x_ref[...], n1_ref[...]).astype(BF16)
    for c0 in range(0, P_DT, PROJ_CHUNK):
        sl = slice(c0, min(c0 + PROJ_CHUNK, P_DT))
        proj_ref[:, sl] = jnp.dot(h_s[...], win_ref[:, sl], preferred_element_type=F32)
    dt_s[...] = jnp.dot(h_s[...], windt_ref[...], preferred_element_type=F32)
    proj_ref[:, P_U:P_TOTAL] = jnp.dot(h_s[...], wintail_ref[...], preferred_element_type=F32)

    def pj(c0, width, rows=slice(None)):
        return proj_ref[rows, c0:c0 + width]

    row = lax.broadcasted_iota(jnp.int32, (tm, 1), 0)
    lane = lax.broadcasted_iota(jnp.int32, (1, LANES), 1)

    ch = pj(P_AC, SC_WIDTH) * pj(P_AH, SC_WIDTH)
    ext = jnp.concatenate([halo_a[...], ch], axis=0)
    scw = scw_ref[...]
    conv_a = (ch * scw[2:3, :]
              + _shift_rows(ext, 1, SUBLANES, tm) * scw[1:2, :]
              + _shift_rows(ext, 2, SUBLANES, tm) * scw[0:1, :])
    halo_a[...] = ch[tm - SUBLANES:tm, :]
    y_s[:, 0:SC_WIDTH] = (pj(P_AB, SC_WIDTH) * conv_a).astype(BF16)

    cw = 512
    for c in range(SSM_CONV_DIM // cw):
        xc = pj(P_XBC + c * cw, cw)
        extx = jnp.concatenate([halo_x[:, c * cw:(c + 1) * cw], xc], axis=0)
        mcw = mcw_ref[:, c * cw:(c + 1) * cw]
        conv = (xc * mcw[3:4, :]
                + _shift_rows(extx, 1, SUBLANES, tm) * mcw[2:3, :]
                + _shift_rows(extx, 2, SUBLANES, tm) * mcw[1:2, :]
                + _shift_rows(extx, 3, SUBLANES, tm) * mcw[0:1, :]
                + mcb_ref[:, c * cw:(c + 1) * cw])
        halo_x[:, c * cw:(c + 1) * cw] = xc[tm - SUBLANES:tm, :]
        xbc_s[:, c * cw:(c + 1) * cw] = _silu(conv)

    tc = SSD_CHUNK
    row_c = lax.broadcasted_iota(jnp.int32, (tc, 1), 0)
    col_c = lax.broadcasted_iota(jnp.int32, (1, tc), 1)
    causal = row_c >= col_c
    head_lane = lane < SSM_HEADS
    neg_a = -jnp.exp(alog_ref[...])
    gw = SSM_D_INNER // SSM_GROUPS
    lo_half = lane < SSM_HEAD_DIM
    for sc in range(tm // tc):
        rows = slice(sc * tc, (sc + 1) * tc)
        dt_in = dt_s[rows, :] + dtb_ref[...]
        dt = jnp.maximum(dt_in, 0.0) + jnp.log1p(jnp.exp(-jnp.abs(dt_in)))
        da = dt * neg_a
        d_hi, d_mid, d_lo = _split3(jnp.where(head_lane, da, 0.0))
        d3 = (d_hi.astype(F32) + pltpu.roll(d_mid.astype(F32), SSM_HEADS, 1)
              + pltpu.roll(d_lo.astype(F32), 2 * SSM_HEADS, 1)).astype(BF16)
        c3 = jnp.dot(causal.astype(BF16), d3, preferred_element_type=F32)
        cs = c3 + pltpu.roll(c3, LANES - SSM_HEADS, 1) + pltpu.roll(c3, LANES - 2 * SSM_HEADS, 1)
        cs = jnp.where(head_lane, cs, 0.0)
        decay_in = jnp.exp(cs)
        to_end = jnp.exp(cs[tc - 1:tc, :] - cs) * dt
        packed = (jnp.where(head_lane, decay_in, 0.0)
                  + pltpu.roll(jnp.where(head_lane, to_end, 0.0), SSM_HEADS, 1))
        hi, mid, lo = _split3(packed)
        x3 = (hi.astype(F32) + pltpu.roll(mid.astype(F32), 2 * SSM_HEADS, 1)
              + pltpu.roll(lo.astype(F32), 4 * SSM_HEADS, 1)).astype(BF16)
        expanded = jnp.dot(x3, rexp_ref[...], preferred_element_type=F32)
        decay_cols = expanded[:, 0:SSM_D_INNER]
        toend_cols = expanded[:, SSM_D_INNER:2 * SSM_D_INNER]

        tpk = jnp.where(head_lane, cs, 0.0) + pltpu.roll(jnp.where(head_lane, dt, 0.0), SSM_HEADS, 1)
        tpk_t = tpk.T

        for g in range(SSM_GROUPS):
            b_g = xbc_s[rows, SSM_D_INNER + g * SSM_STATE:SSM_D_INNER + (g + 1) * SSM_STATE].astype(BF16)
            c_off = SSM_D_INNER + SSM_GROUPS * SSM_STATE
            c_g = xbc_s[rows, c_off + g * SSM_STATE:c_off + (g + 1) * SSM_STATE].astype(BF16)
            cb = lax.dot_general(c_g, b_g, (((1,), (1,)), ((), ())), preferred_element_type=F32)
            s_g = state[g * SSM_STATE:(g + 1) * SSM_STATE, :]
            y_off = (jnp.dot(c_g, s_g.astype(BF16), preferred_element_type=F32)
                     * decay_cols[:, g * gw:(g + 1) * gw])
            xs_g = xbc_s[rows, g * gw:(g + 1) * gw]
            pieces = []
            for jj in range(2):
                xs_t = xs_g[:, jj * LANES:(jj + 1) * LANES].astype(BF16)
                acc = None
                for kk in range(2):
                    h = g * 4 + jj * 2 + kk
                    seg = jnp.exp(jnp.where(causal, cs[:, h:h + 1] - tpk_t[h:h + 1, :], -jnp.inf))
                    scores = (cb * seg * tpk_t[SSM_HEADS + h:SSM_HEADS + h + 1, :]).astype(BF16)
                    keep = lo_half if kk == 0 else jnp.logical_not(lo_half)
                    part = jnp.dot(scores, jnp.where(keep, xs_t, jnp.zeros_like(xs_t)),
                                   preferred_element_type=F32)
                    acc = part if acc is None else acc + part
                pieces.append(acc)
            y_g = jnp.concatenate(pieces, axis=1) + y_off + xs_g * dskip_ref[:, g * gw:(g + 1) * gw]
            upd = jnp.dot(b_g.T, (toend_cols[:, g * gw:(g + 1) * gw] * xs_g).astype(BF16),
                          preferred_element_type=F32)
            state[g * SSM_STATE:(g + 1) * SSM_STATE, :] = (
                decay_cols[tc - 1:tc, g * gw:(g + 1) * gw] * s_g + upd)
            y_g = y_g * _silu(pj(P_Z + g * gw, gw, rows))
            y_g = y_g * lax.rsqrt(jnp.mean(y_g * y_g, axis=-1, keepdims=True) + EPS)
            y_s[rows, SC_WIDTH + g * gw:SC_WIDTH + (g + 1) * gw] = (
                y_g * ng_ref[:, g * gw:(g + 1) * gw]).astype(BF16)

    def gelu(t):
        return 0.5 * t * (1.0 + jnp.tanh(0.7978845608028654 * (t + 0.044715 * (t * t * t))))

    v = gelu(pj(P_V, GM_WIDTH))
    mu = jnp.mean(v, axis=-1, keepdims=True)
    vc = v - mu
    var = jnp.mean(vc * vc, axis=-1, keepdims=True)
    vf = (vc * lax.rsqrt(var + EPS) * lng_ref[...] + lnb_ref[...]).astype(BF16)
    r_b = lax.broadcasted_iota(jnp.int32, (GM_BLOCK, 1), 0)
    c_b = lax.broadcasted_iota(jnp.int32, (1, GM_BLOCK), 1)
    gc = GM_WIDTH // GM_GROUPS
    for g in range(GM_GROUPS):
        wsm = jnp.where(r_b >= c_b, gws_ref[g], 0.0).astype(BF16)
        blocks = [jnp.dot(wsm, vf[b * GM_BLOCK:(b + 1) * GM_BLOCK, g * gc:(g + 1) * gc],
                          preferred_element_type=F32) + gbias_ref[:, g * gc:(g + 1) * gc]
                  for b in range(tm // GM_BLOCK)]
        s_sp = jnp.concatenate(blocks, axis=0)
        u_g = gelu(pj(P_U + g * gc, gc))
        off = SC_WIDTH + SSM_D_INNER
        y_s[:, off + g * gc:off + (g + 1) * gc] = (u_g * s_sp).astype(BF16)

    t_glob = i * tm + row
    pc = POOL_WIDTH // len(POOL_WINDOWS)
    for g, w in enumerate(POOL_WINDOWS):
        pd_g = pj(P_PD + g * pc, pc)
        s = jnp.concatenate([halo_p[:, g * pc:(g + 1) * pc], pd_g], axis=0)
        k = 1
        while k < w:
            s = s + pltpu.roll(s, k, 0)
            k *= 2
        win = s[POOL_HALO:POOL_HALO + tm]
        inv_cnt = 1.0 / jnp.minimum(t_glob + 1, w).astype(F32)
        pooled = (win * inv_cnt - pd_g).astype(BF16)
        halo_p[:, g * pc:(g + 1) * pc] = pd_g[tm - POOL_HALO:tm, :]
        off = SC_WIDTH + SSM_D_INNER + GM_WIDTH
        y_d = jnp.dot(pooled, pmap_ref[g], preferred_element_type=F32) * pscale_ref[:, g * pc:(g + 1) * pc]
        y_s[:, off + g * pc:off + (g + 1) * pc] = y_d.astype(BF16)

    bounds = (0, SC_WIDTH, SC_WIDTH + SSM_D_INNER, SC_WIDTH + SSM_D_INNER + GM_WIDTH, BR_TOTAL)
    merged = None
    for b, wbr_ref in enumerate((wbra_ref, wbrb_ref, wbrc_ref, wbrd_ref)):
        br = jnp.dot(y_s[:, bounds[b]:bounds[b + 1]], wbr_ref[...], preferred_element_type=F32)
        gate = jax.nn.sigmoid(jnp.dot(h_s[...], wgate_ref[:, b * D_MODEL:(b + 1) * D_MODEL],
                                      preferred_element_type=F32))
        term = gate * br
        merged = term if merged is None else merged + term
    o_ref[...] = x_ref[...] + jnp.dot(merged.astype(BF16), wout_ref[...], preferred_element_type=F32)


def _head_expand_matrix():
    r = np.zeros((LANES, 2 * SSM_D_INNER), np.float32)
    for piece in range(3):
        for q in range(2):
            for h in range(SSM_HEADS):
                rr = piece * 2 * SSM_HEADS + q * SSM_HEADS + h
                r[rr, q * SSM_D_INNER + h * SSM_HEAD_DIM:q * SSM_D_INNER + (h + 1) * SSM_HEAD_DIM] = 1.0
    return jnp.asarray(r, BF16)


def _mixer(x, p):
    L = x.shape[0]
    tm = TM_MIX
    full = lambda a: pl.BlockSpec(a.shape, lambda i, _n=a.ndim: (0,) * _n, pipeline_mode=pl.Buffered(1))
    consts = [p["norm1"], p["w_in"], p["w_in_dt"], p["w_in_tail"], p["w_gate"], p["scw"], p["mcw"], p["mcb"],
              p["dtb"], p["alog"], p["dskip"], p["ng"], p["lng"], p["lnb"], p["gws"], p["gbias"], p["pmap"],
              p["pscale"], p["rexp"], p["wbr_a"], p["wbr_b"], p["wbr_c"], p["wbr_d"], p["wout"]]
    est = (4 * tm * D_MODEL * 4 + sum(int(a.size) * a.dtype.itemsize for a in consts)
           + tm * (D_MODEL * 2 + P_TOTAL * 4 + LANES * 4 + SSM_CONV_DIM * 4 + BR_TOTAL * 2) + 12 * tm * D_MODEL * 4)
    return pl.pallas_call(
        _mixer_kernel,
        grid=(L // tm,),
        in_specs=[pl.BlockSpec((tm, D_MODEL), lambda i: (i, 0))] + [full(a) for a in consts],
        out_specs=pl.BlockSpec((tm, D_MODEL), lambda i: (i, 0)),
        out_shape=jax.ShapeDtypeStruct((L, D_MODEL), F32),
        scratch_shapes=[
            pltpu.VMEM((SUBLANES, SC_WIDTH), F32),
            pltpu.VMEM((SUBLANES, SSM_CONV_DIM), F32),
            pltpu.VMEM((POOL_HALO, POOL_WIDTH), F32),
            pltpu.VMEM((SSM_GROUPS * SSM_STATE, SSM_D_INNER // SSM_GROUPS), F32),
            pltpu.VMEM((tm, D_MODEL), BF16),
            pltpu.VMEM((tm, P_TOTAL), F32),
            pltpu.VMEM((tm, LANES), F32),
            pltpu.VMEM((tm, SSM_CONV_DIM), F32),
            pltpu.VMEM((tm, BR_TOTAL), BF16),
        ],
        compiler_params=pltpu.CompilerParams(
            dimension_semantics=("arbitrary",), vmem_limit_bytes=_vmem_limit(est)),
        name="mixers",
    )(x, *consts)


def _ffn_kernel(x_ref, g_ref, wg_ref, wu_ref, wd_ref, o_ref):
    x = x_ref[...]
    h = _rms(x, g_ref[...]).astype(BF16)
    acc = x
    for c in range(wg_ref.shape[1] // FFN_CHUNK):
        sl = slice(c * FFN_CHUNK, (c + 1) * FFN_CHUNK)
        gate = jnp.dot(h, wg_ref[:, sl], preferred_element_type=F32)
        up = jnp.dot(h, wu_ref[:, sl], preferred_element_type=F32)
        acc = acc + jnp.dot((_silu(gate) * up).astype(BF16), wd_ref[sl, :], preferred_element_type=F32)
    o_ref[...] = acc


def _dense_ffn(x, g, wg, wu, wd):
    L, d = x.shape
    f = wg.shape[1]
    tm = TM_FFN
    est = 4 * tm * d * 4 + 3 * d * f * 2 + 4 * tm * FFN_CHUNK * 4
    return pl.pallas_call(
        _ffn_kernel,
        grid=(L // tm,),
        in_specs=[pl.BlockSpec((tm, d), lambda i: (i, 0)),
                  pl.BlockSpec((1, d), lambda i: (0, 0)),
                  pl.BlockSpec((d, f), lambda i: (0, 0), pipeline_mode=pl.Buffered(1)),
                  pl.BlockSpec((d, f), lambda i: (0, 0), pipeline_mode=pl.Buffered(1)),
                  pl.BlockSpec((f, d), lambda i: (0, 0), pipeline_mode=pl.Buffered(1))],
        out_specs=pl.BlockSpec((tm, d), lambda i: (i, 0)),
        out_shape=jax.ShapeDtypeStruct((L, d), F32),
        compiler_params=pltpu.CompilerParams(
            dimension_semantics=("parallel",), vmem_limit_bytes=_vmem_limit(est)),
        name="dense_swiglu",
    )(x, g, wg, wu, wd)


def _router_kernel(x_ref, g_ref, rw_ref, ri_ref, rg_ref, cnt_ref, carry):
    i = pl.program_id(0)
    tm = x_ref.shape[0]

    @pl.when(i == 0)
    def _():
        carry[...] = jnp.zeros_like(carry)

    h = _rms(x_ref[...], g_ref[...])
    pieces = jnp.dot(jnp.concatenate(_split3(h), axis=0), rw_ref[...], preferred_element_type=F32)
    s3 = pieces[0:tm] + pieces[tm:2 * tm] + pieces[2 * tm:3 * tm]
    logits = s3 + pltpu.roll(s3, LANES - N_EXPERTS, 1) + pltpu.roll(s3, LANES - 2 * N_EXPERTS, 1)
    lane = lax.broadcasted_iota(jnp.int32, (1, LANES), 1).astype(F32)
    logits = jnp.where(lane < N_EXPERTS, logits, -jnp.inf)
    m1 = jnp.max(logits, axis=-1, keepdims=True)
    i1 = jnp.min(jnp.where(logits == m1, lane, float(LANES)), axis=-1, keepdims=True)
    rest = jnp.where(lane == i1, -jnp.inf, logits)
    m2 = jnp.max(rest, axis=-1, keepdims=True)
    i2 = jnp.min(jnp.where(rest == m2, lane, float(LANES)), axis=-1, keepdims=True)
    e21 = jnp.exp(m2 - m1)
    g1 = 1.0 / (1.0 + e21)
    g2 = e21 / (1.0 + e21)

    hot1 = lane == i1
    hot2 = lane == i2
    onehot = jnp.logical_or(hot1, hot2).astype(BF16)
    row = lax.broadcasted_iota(jnp.int32, (tm, 1), 0)
    colt = lax.broadcasted_iota(jnp.int32, (1, tm), 1)
    before = (row > colt).astype(BF16)
    seen = jnp.dot(before, onehot, preferred_element_type=F32) + carry[...]
    r1 = jnp.sum(jnp.where(hot1, seen, 0.0), axis=-1, keepdims=True)
    r2 = jnp.sum(jnp.where(hot2, seen, 0.0), axis=-1, keepdims=True)
    carry[...] = carry[...] + jnp.sum(onehot.astype(F32), axis=0, keepdims=True)

    ri = jnp.where(lane == 0, i1, jnp.where(lane == 1, i2, jnp.where(lane == 2, r1, r2)))
    ri_ref[...] = ri.astype(jnp.int32)
    rg_ref[...] = jnp.where(lane == 0, g1, g2)
    cnt_ref[...] = carry[...]


def _router(x, g, rw):
    L, d = x.shape
    tm = TM_ROUTER
    return pl.pallas_call(
        _router_kernel,
        grid=(L // tm,),
        in_specs=[pl.BlockSpec((tm, d), lambda i: (i, 0)),
                  pl.BlockSpec((1, d), lambda i: (0, 0)),
                  pl.BlockSpec((d, LANES), lambda i: (0, 0))],
        out_specs=[pl.BlockSpec((tm, LANES), lambda i: (i, 0)),
                   pl.BlockSpec((tm, LANES), lambda i: (i, 0)),
                   pl.BlockSpec((1, LANES), lambda i: (0, 0))],
        out_shape=[jax.ShapeDtypeStruct((L, LANES), jnp.int32),
                   jax.ShapeDtypeStruct((L, LANES), F32),
                   jax.ShapeDtypeStruct((1, LANES), F32)],
        scratch_shapes=[pltpu.VMEM((1, LANES), F32)],
        compiler_params=pltpu.CompilerParams(dimension_semantics=("arbitrary",)),
        name="router_top2",
    )(x, g, rw)


def _row_copy(src, dst, sem, s, d):
    return pltpu.make_async_copy(src.at[pl.ds(s, 1), :], dst.at[pl.ds(d, 1), :], sem)


def _start_then_wait_row_copies(tm, copy_of):
    def start(t, c):
        for k in range(TOP_K):
            copy_of(t, k).start(priority=k)
        return c

    def wait(t, c):
        for k in range(TOP_K):
            copy_of(t, k).wait()
        return c

    lax.fori_loop(0, tm, start, 0, unroll=DMA_UNROLL)
    lax.fori_loop(0, tm, wait, 0, unroll=DMA_UNROLL)


def _dispatch_kernel(pos_ref, tail_ref, x_ref, g_ref, xs_ref, h_s, sem):
    tm = x_ref.shape[0]

    @pl.when(pl.program_id(0) == 0)
    def _clear_unwritten_tiles():
        h_s[...] = jnp.zeros_like(h_s)
        n_tiles = xs_ref.shape[0] // TM_GROUP
        tiles = [(tail_ref[e] >= 0, tail_ref[e]) for e in range(N_EXPERTS)]
        tiles += [(t >= tail_ref[N_EXPERTS], t * TM_GROUP) for t in range(n_tiles - N_EXPERTS, n_tiles)]

        def clear_copy(first_row, j):
            first = first_row + j * tm
            if not isinstance(first, int):
                first = pl.multiple_of(first, tm)
            return pltpu.make_async_copy(h_s, xs_ref.at[pl.ds(first, tm), :], sem)

        for action in ("start", "wait"):
            for needed, first_row in tiles:
                @pl.when(needed)
                def _(first_row=first_row, action=action):
                    for j in range(TM_GROUP // tm):
                        getattr(clear_copy(first_row, j), action)()

    h_s[...] = _rms(x_ref[...], g_ref[...])
    _start_then_wait_row_copies(
        tm, lambda t, k: _row_copy(h_s, xs_ref, sem, t, pos_ref[0, 0, TOP_K * t + k]))


def _dispatch(x, g, pos, tail_rows, n_sorted):
    L, d = x.shape
    tm = TM_ROUTE
    assert TM_GROUP % tm == 0
    pos3 = pos.reshape(L // tm, 1, TOP_K * tm)
    return pl.pallas_call(
        _dispatch_kernel,
        grid=(L // tm,),
        in_specs=[pl.BlockSpec((1, 1, TOP_K * tm), lambda i: (i, 0, 0), memory_space=pltpu.SMEM),
                  pl.BlockSpec(memory_space=pltpu.SMEM),
                  pl.BlockSpec((tm, d), lambda i: (i, 0)),
                  pl.BlockSpec((1, d), lambda i: (0, 0))],
        out_specs=pl.BlockSpec(memory_space=pl.ANY),
        out_shape=jax.ShapeDtypeStruct((n_sorted, d), F32),
        scratch_shapes=[pltpu.VMEM((tm, d), F32), pltpu.SemaphoreType.DMA(())],
        compiler_params=pltpu.CompilerParams(dimension_semantics=("arbitrary",)),
        name="moe_dispatch",
    )(pos3, tail_rows, x, g)


def _expert_kernel(te_ref, tv_ref, tb_ref, xs_ref, wg_ref, wu_ref, wd_ref, ys_ref):
    del tb_ref
    i = pl.program_id(0)

    @pl.when(tv_ref[i] == 0)
    def _():
        ys_ref[...] = jnp.zeros_like(ys_ref)

    @pl.when(tv_ref[i] != 0)
    def _():
        x = xs_ref[...].astype(BF16)
        acc = None
        for c in range(wg_ref.shape[2] // MOE_CHUNK):
            sl = slice(c * MOE_CHUNK, (c + 1) * MOE_CHUNK)
            gate = jnp.dot(x, wg_ref[0, :, sl], preferred_element_type=F32)
            up = jnp.dot(x, wu_ref[0, :, sl], preferred_element_type=F32)
            part = jnp.dot((_silu(gate) * up).astype(BF16), wd_ref[0, sl, :], preferred_element_type=F32)
            acc = part if acc is None else acc + part
        ys_ref[...] = acc


def _experts(xs, tile_expert, tile_valid, tile_block, wg, wu, wd):
    n_sorted, d = xs.shape
    f = wg.shape[2]
    tm = TM_GROUP
    single = pl.Buffered(1)
    est = 3 * d * f * 2 + 4 * tm * d * 4 + 4 * tm * MOE_CHUNK * 4
    return pl.pallas_call(
        _expert_kernel,
        grid_spec=pltpu.PrefetchScalarGridSpec(
            num_scalar_prefetch=3,
            grid=(n_sorted // tm,),
            in_specs=[pl.BlockSpec((tm, d), lambda i, te, tv, tb: (tb[i], 0)),
                      pl.BlockSpec((1, d, f), lambda i, te, tv, tb: (te[i], 0, 0), pipeline_mode=single),
                      pl.BlockSpec((1, d, f), lambda i, te, tv, tb: (te[i], 0, 0), pipeline_mode=single),
                      pl.BlockSpec((1, f, d), lambda i, te, tv, tb: (te[i], 0, 0), pipeline_mode=single)],
            out_specs=pl.BlockSpec((tm, d), lambda i, te, tv, tb: (i, 0)),
        ),
        out_shape=jax.ShapeDtypeStruct((n_sorted, d), F32),
        compiler_params=pltpu.CompilerParams(
            dimension_semantics=("arbitrary",), vmem_limit_bytes=_vmem_limit(est)),
        name="expert_swiglu",
    )(tile_expert, tile_valid, tile_block, xs, wg, wu, wd)


def _combine_kernel(pos_ref, x_ref, rg_ref, fg_ref, ys_ref, o_ref, buf, sem):
    tm = x_ref.shape[0]
    _start_then_wait_row_copies(
        tm, lambda t, k: _row_copy(ys_ref, buf.at[k], sem, pos_ref[0, 0, TOP_K * t + k], t))
    rg = rg_ref[...]
    y = x_ref[...] + rg[:, 0:1] * buf[0] + rg[:, 1:2] * buf[1]
    o_ref[...] = _rms(y, fg_ref[...])


def _combine(x, rg, pos, ys, final_g):
    L, d = x.shape
    tm = TM_ROUTE
    pos3 = pos.reshape(L // tm, 1, TOP_K * tm)
    return pl.pallas_call(
        _combine_kernel,
        grid=(L // tm,),
        in_specs=[pl.BlockSpec((1, 1, TOP_K * tm), lambda i: (i, 0, 0), memory_space=pltpu.SMEM),
                  pl.BlockSpec((tm, d), lambda i: (i, 0)),
                  pl.BlockSpec((tm, LANES), lambda i: (i, 0)),
                  pl.BlockSpec((1, d), lambda i: (0, 0)),
                  pl.BlockSpec(memory_space=pl.ANY)],
        out_specs=pl.BlockSpec((tm, d), lambda i: (i, 0)),
        out_shape=jax.ShapeDtypeStruct((L, d), F32),
        scratch_shapes=[pltpu.VMEM((TOP_K, tm, d), F32), pltpu.SemaphoreType.DMA(())],
        compiler_params=pltpu.CompilerParams(dimension_semantics=("arbitrary",)),
        name="moe_combine_norm",
    )(pos3, x, rg, final_g, ys)


def _moe_ffn_final_norm(x, g2, router_w, wg, wu, wd, final_g):
    L, d = x.shape
    rw_hi = router_w.astype(BF16)
    rw_mid = (router_w - rw_hi.astype(F32)).astype(BF16)
    rw_lo = (router_w - rw_hi.astype(F32) - rw_mid.astype(F32)).astype(BF16)
    rw = jnp.pad(jnp.concatenate([rw_hi, rw_mid, rw_lo], axis=1), ((0, 0), (0, LANES - 3 * N_EXPERTS)))
    ri, rg, cnt = _router(x, g2, rw)
    counts = cnt[0, :N_EXPERTS].astype(jnp.int32)
    padded = ((counts + TM_GROUP - 1) // TM_GROUP) * TM_GROUP
    ends = jnp.cumsum(padded)
    starts = ends - padded
    pos = starts[ri[:, 0:TOP_K]] + ri[:, TOP_K:2 * TOP_K]
    n_sorted = TOP_K * L + N_EXPERTS * TM_GROUP
    tile_row = jnp.arange(n_sorted // TM_GROUP, dtype=jnp.int32) * TM_GROUP
    tile_valid = (tile_row < ends[-1]).astype(jnp.int32)
    last_row = jnp.minimum(tile_row, ends[-1] - TM_GROUP)
    tile_expert = jnp.sum((ends[None, :] <= last_row[:, None]).astype(jnp.int32), axis=1)
    tile_expert = jnp.minimum(tile_expert, N_EXPERTS - 1)
    tile_block = last_row // TM_GROUP
    tail_rows = jnp.concatenate(
        [jnp.where(padded > 0, ends - TM_GROUP, -1), ends[-1:] // TM_GROUP]).astype(jnp.int32)

    xs = _dispatch(x, g2, pos, tail_rows, n_sorted)
    ys = _experts(xs, tile_expert, tile_valid, tile_block, wg.astype(BF16), wu.astype(BF16), wd.astype(BF16))
    return _combine(x, rg, pos, ys, final_g.reshape(1, d))


def kernel(x, norm1_g, w_in, w_gate, sc_conv_w, ssm_conv_w, ssm_conv_b, ssm_dt_bias, ssm_a_log, ssm_d, ssm_norm_g, gm_ln_g, gm_ln_b, gm_ws, gm_bias, pool_map, pool_scale, w_br_a, w_br_b, w_br_c, w_br_d, w_out, norm2_g, ffn_wg, ffn_wu, ffn_wd, router_w, moe_wg, moe_wu, moe_wd, final_g):
    bsz, L, d = x.shape
    assert bsz == 1 and d == D_MODEL and L % TM_FFN == 0
    depth = norm1_g.shape[0]
    assert depth == 2, "layer 0 uses the dense SwiGLU, layer 1 the expert SwiGLU followed by the final norm"
    xt = x.reshape(L, d)
    rexp = _head_expand_matrix()
    pad_h = LANES - SSM_HEADS
    for layer in range(depth):
        w_in_l = w_in[layer]
        p = {
            "norm1": norm1_g[layer].reshape(1, d),
            "w_in": w_in_l[:, :P_DT].astype(BF16),
            "w_in_dt": jnp.pad(w_in_l[:, P_DT:IN_DT_END], ((0, 0), (0, pad_h))).astype(BF16),
            "w_in_tail": w_in_l[:, IN_DT_END:].astype(BF16),
            "w_gate": w_gate[layer].astype(BF16),
            "scw": sc_conv_w[layer],
            "mcw": ssm_conv_w[layer],
            "mcb": ssm_conv_b[layer].reshape(1, -1),
            "dtb": jnp.pad(ssm_dt_bias[layer], (0, pad_h)).reshape(1, LANES),
            "alog": jnp.pad(ssm_a_log[layer], (0, pad_h)).reshape(1, LANES),
            "dskip": jnp.repeat(ssm_d[layer], SSM_HEAD_DIM).reshape(1, SSM_D_INNER),
            "ng": ssm_norm_g[layer].reshape(1, -1),
            "lng": gm_ln_g[layer].reshape(1, -1),
            "lnb": gm_ln_b[layer].reshape(1, -1),
            "gws": gm_ws[layer],
            "gbias": jnp.repeat(gm_bias[layer].T, GM_WIDTH // GM_GROUPS, axis=1),
            "pmap": pool_map[layer].astype(BF16),
            "pscale": pool_scale[layer].reshape(1, -1),
            "rexp": rexp,
            "wbr_a": w_br_a[layer].astype(BF16),
            "wbr_b": w_br_b[layer].astype(BF16),
            "wbr_c": w_br_c[layer].astype(BF16),
            "wbr_d": w_br_d[layer].astype(BF16),
            "wout": w_out[layer].astype(BF16),
        }
        xt = _mixer(xt, p)
        idx = layer // 2
        g2 = norm2_g[layer].reshape(1, d)
        if layer % 2 == 0:
            xt = _dense_ffn(xt, g2, ffn_wg[idx].astype(BF16), ffn_wu[idx].astype(BF16), ffn_wd[idx].astype(BF16))
        else:
            xt = _moe_ffn_final_norm(xt, g2, router_w[idx], moe_wg[idx], moe_wu[idx], moe_wd[idx], final_g)
    return xt.reshape(bsz, L, d)
```

```python
import numpy as np
import jax
import jax.numpy as jnp
from jax import lax
from jax.experimental import pallas as pl
from jax.experimental.pallas import tpu as pltpu

F32 = jnp.float32
BF16 = jnp.bfloat16
EPS = 1e-6

LANES = 128
SUBLANES = 8

D_MODEL = 1024
SC_WIDTH = 512
SSM_D_INNER = 1024
SSM_HEAD_DIM = 64
SSM_HEADS = 16
SSM_GROUPS = 4
SSM_STATE = 128
SSM_CONV_DIM = 2048
GM_WIDTH = 512
GM_GROUPS = 4
GM_BLOCK = 128
POOL_WIDTH = 512
POOL_WINDOWS = (2, 4, 8, 16)
POOL_HALO = 16
N_BRANCH = 4
N_EXPERTS = 8
TOP_K = 2

P_AH, P_AB, P_AC = 0, 512, 1024
P_Z = 1536
P_XBC = 2560
P_DT = 4608
P_U = P_DT + LANES
P_V = P_U + GM_WIDTH
P_PD = P_V + GM_WIDTH
P_TOTAL = P_PD + POOL_WIDTH
IN_DT_END = 4624

BR_TOTAL = SC_WIDTH + SSM_D_INNER + GM_WIDTH + POOL_WIDTH

TM_MIX = 256
SSD_CHUNK = 256
PROJ_CHUNK = 1792
TM_FFN = 512
TM_ROUTER = 512
TM_ROUTE = 256
TM_GROUP = 512
FFN_CHUNK = 256
MOE_CHUNK = 256
DMA_UNROLL = 8


def _vmem_limit(nbytes):
    return int(min(nbytes + (8 << 20), 60 << 20))


def _rms(x, g):
    return x * lax.rsqrt(jnp.mean(x * x, axis=-1, keepdims=True) + EPS) * g


def _silu(x):
    return x * jax.nn.sigmoid(x)


def _shift_rows(ext, k, halo, tm):
    return pltpu.roll(ext, k, 0)[halo:halo + tm]


def _split3(x):
    hi = x.astype(BF16)
    r1 = x - hi.astype(F32)
    mid = r1.astype(BF16)
    lo = (r1 - mid.astype(F32)).astype(BF16)
    return hi, mid, lo


def _mixer_kernel(x_ref, n1_ref, win_ref, windt_ref, wintail_ref, wgate_ref, scw_ref, mcw_ref, mcb_ref,
                  dtb_ref, alog_ref, dskip_ref, ng_ref, lng_ref, lnb_ref, gws_ref, gbias_ref, pmap_ref,
                  pscale_ref, rexp_ref, wbra_ref, wbrb_ref, wbrc_ref, wbrd_ref, wout_ref, o_ref,
                  halo_a, halo_x, halo_p, state, h_s, proj_ref, dt_s, xbc_s, y_s):
    i = pl.program_id(0)
    tm = x_ref.shape[0]

    @pl.when(i == 0)
    def _init():
        halo_a[...] = jnp.zeros_like(halo_a)
        halo_x[...] = jnp.zeros_like(halo_x)
        halo_p[...] = jnp.zeros_like(halo_p)
        state[...] = jnp.zeros_like(state)

    h_s[...] = _rms(x_ref[...], n1_ref[...]).astype(BF16)
    for c0 in range(0, P_DT, PROJ_CHUNK):
        sl = slice(c0, min(c0 + PROJ_CHUNK, P_DT))
        proj_ref[:, sl] = jnp.dot(h_s[...], win_ref[:, sl], preferred_element_type=F32)
    dt_s[...] = jnp.dot(h_s[...], windt_ref[...], preferred_element_type=F32)
    proj_ref[:, P_U:P_TOTAL] = jnp.dot(h_s[...], wintail_ref[...], preferred_element_type=F32)

    def pj(c0, width, rows=slice(None)):
        return proj_ref[rows, c0:c0 + width]

    row = lax.broadcasted_iota(jnp.int32, (tm, 1), 0)
    lane = lax.broadcasted_iota(jnp.int32, (1, LANES), 1)

    ch = pj(P_AC, SC_WIDTH) * pj(P_AH, SC_WIDTH)
    ext = jnp.concatenate([halo_a[...], ch], axis=0)
    scw = scw_ref[...]
    conv_a = (ch * scw[2:3, :]
              + _shift_rows(ext, 1, SUBLANES, tm) * scw[1:2, :]
              + _shift_rows(ext, 2, SUBLANES, tm) * scw[0:1, :])
    halo_a[...] = ch[tm - SUBLANES:tm, :]
    y_s[:, 0:SC_WIDTH] = (pj(P_AB, SC_WIDTH) * conv_a).astype(BF16)

    cw = 512
    for c in range(SSM_CONV_DIM // cw):
        xc = pj(P_XBC + c * cw, cw)
        extx = jnp.concatenate([halo_x[:, c * cw:(c + 1) * cw], xc], axis=0)
        mcw = mcw_ref[:, c * cw:(c + 1) * cw]
        conv = (xc * mcw[3:4, :]
                + _shift_rows(extx, 1, SUBLANES, tm) * mcw[2:3, :]
                + _shift_rows(extx, 2, SUBLANES, tm) * mcw[1:2, :]
                + _shift_rows(extx, 3, SUBLANES, tm) * mcw[0:1, :]
                + mcb_ref[:, c * cw:(c + 1) * cw])
        halo_x[:, c * cw:(c + 1) * cw] = xc[tm - SUBLANES:tm, :]
        xbc_s[:, c * cw:(c + 1) * cw] = _silu(conv)

    tc = SSD_CHUNK
    row_c = lax.broadcasted_iota(jnp.int32, (tc, 1), 0)
    col_c = lax.broadcasted_iota(jnp.int32, (1, tc), 1)
    causal = row_c >= col_c
    head_lane = lane < SSM_HEADS
    neg_a = -jnp.exp(alog_ref[...])
    gw = SSM_D_INNER // SSM_GROUPS
    lo_half = lane < SSM_HEAD_DIM
    for sc in range(tm // tc):
        rows = slice(sc * tc, (sc + 1) * tc)
        dt_in = dt_s[rows, :] + dtb_ref[...]
        dt = jnp.maximum(dt_in, 0.0) + jnp.log1p(jnp.exp(-jnp.abs(dt_in)))
        da = dt * neg_a
        d_hi, d_mid, d_lo = _split3(jnp.where(head_lane, da, 0.0))
        d3 = (d_hi.astype(F32) + pltpu.roll(d_mid.astype(F32), SSM_HEADS, 1)
              + pltpu.roll(d_lo.astype(F32), 2 * SSM_HEADS, 1)).astype(BF16)
        c3 = jnp.dot(causal.astype(BF16), d3, preferred_element_type=F32)
        cs = c3 + pltpu.roll(c3, LANES - SSM_HEADS, 1) + pltpu.roll(c3, LANES - 2 * SSM_HEADS, 1)
        cs = jnp.where(head_lane, cs, 0.0)
        decay_in = jnp.exp(cs)
        to_end = jnp.exp(cs[tc - 1:tc, :] - cs) * dt
        packed = (jnp.where(head_lane, decay_in, 0.0)
                  + pltpu.roll(jnp.where(head_lane, to_end, 0.0), SSM_HEADS, 1))
        hi, mid, lo = _split3(packed)
        x3 = (hi.astype(F32) + pltpu.roll(mid.astype(F32), 2 * SSM_HEADS, 1)
              + pltpu.roll(lo.astype(F32), 4 * SSM_HEADS, 1)).astype(BF16)
        expanded = jnp.dot(x3, rexp_ref[...], preferred_element_type=F32)
        decay_cols = expanded[:, 0:SSM_D_INNER]
        toend_cols = expanded[:, SSM_D_INNER:2 * SSM_D_INNER]

        tpk = jnp.where(head_lane, cs, 0.0) + pltpu.roll(jnp.where(head_lane, dt, 0.0), SSM_HEADS, 1)
        tpk_t = tpk.T

        for g in range(SSM_GROUPS):
            b_g = xbc_s[rows, SSM_D_INNER + g * SSM_STATE:SSM_D_INNER + (g + 1) * SSM_STATE].astype(BF16)
            c_off = SSM_D_INNER + SSM_GROUPS * SSM_STATE
            c_g = xbc_s[rows, c_off + g * SSM_STATE:c_off + (g + 1) * SSM_STATE].astype(BF16)
            cb = lax.dot_general(c_g, b_g, (((1,), (1,)), ((), ())), preferred_element_type=F32)
            s_g = state[g * SSM_STATE:(g + 1) * SSM_STATE, :]
            y_off = (jnp.dot(c_g, s_g.astype(BF16), preferred_element_type=F32)
                     * decay_cols[:, g * gw:(g + 1) * gw])
            xs_g = xbc_s[rows, g * gw:(g + 1) * gw]
            pieces = []
            for jj in range(2):
                xs_t = xs_g[:, jj * LANES:(jj + 1) * LANES].astype(BF16)
                acc = None
                for kk in range(2):
                    h = g * 4 + jj * 2 + kk
                    seg = jnp.exp(jnp.where(causal, cs[:, h:h + 1] - tpk_t[h:h + 1, :], -jnp.inf))
                    scores = (cb * seg * tpk_t[SSM_HEADS + h:SSM_HEADS + h + 1, :]).astype(BF16)
                    keep = lo_half if kk == 0 else jnp.logical_not(lo_half)
                    part = jnp.dot(scores, jnp.where(keep, xs_t, jnp.zeros_like(xs_t)),
                                   preferred_element_type=F32)
                    acc = part if acc is None else acc + part
                pieces.append(acc)
            y_g = jnp.concatenate(pieces, axis=1) + y_off + xs_g * dskip_ref[:, g * gw:(g + 1) * gw]
            upd = jnp.dot(b_g.T, (toend_cols[:, g * gw:(g + 1) * gw] * xs_g).astype(BF16),
                          preferred_element_type=F32)
            state[g * SSM_STATE:(g + 1) * SSM_STATE, :] = (
                decay_cols[tc - 1:tc, g * gw:(g + 1) * gw] * s_g + upd)
            y_g = y_g * _silu(pj(P_Z + g * gw, gw, rows))
            y_g = y_g * lax.rsqrt(jnp.mean(y_g * y_g, axis=-1, keepdims=True) + EPS)
            y_s[rows, SC_WIDTH + g * gw:SC_WIDTH + (g + 1) * gw] = (
                y_g * ng_ref[:, g * gw:(g + 1) * gw]).astype(BF16)

    def gelu(t):
        return 0.5 * t * (1.0 + jnp.tanh(0.7978845608028654 * (t + 0.044715 * (t * t * t))))

    v = gelu(pj(P_V, GM_WIDTH))
    mu = jnp.mean(v, axis=-1, keepdims=True)
    vc = v - mu
    var = jnp.mean(vc * vc, axis=-1, keepdims=True)
    vf = (vc * lax.rsqrt(var + EPS) * lng_ref[...] + lnb_ref[...]).astype(BF16)
    r_b = lax.broadcasted_iota(jnp.int32, (GM_BLOCK, 1), 0)
    c_b = lax.broadcasted_iota(jnp.int32, (1, GM_BLOCK), 1)
    gc = GM_WIDTH // GM_GROUPS
    for g in range(GM_GROUPS):
        wsm = jnp.where(r_b >= c_b, gws_ref[g], 0.0).astype(BF16)
        blocks = [jnp.dot(wsm, vf[b * GM_BLOCK:(b + 1) * GM_BLOCK, g * gc:(g + 1) * gc],
                          preferred_element_type=F32) + gbias_ref[:, g * gc:(g + 1) * gc]
                  for b in range(tm // GM_BLOCK)]
        s_sp = jnp.concatenate(blocks, axis=0)
        u_g = gelu(pj(P_U + g * gc, gc))
        off = SC_WIDTH + SSM_D_INNER
        y_s[:, off + g * gc:off + (g + 1) * gc] = (u_g * s_sp).astype(BF16)

    t_glob = i * tm + row
    pc = POOL_WIDTH // len(POOL_WINDOWS)
    for g, w in enumerate(POOL_WINDOWS):
        pd_g = pj(P_PD + g * pc, pc)
        s = jnp.concatenate([halo_p[:, g * pc:(g + 1) * pc], pd_g], axis=0)
        k = 1
        while k < w:
            s = s + pltpu.roll(s, k, 0)
            k *= 2
        win = s[POOL_HALO:POOL_HALO + tm]
        inv_cnt = 1.0 / jnp.minimum(t_glob + 1, w).astype(F32)
        pooled = (win * inv_cnt - pd_g).astype(BF16)
        halo_p[:, g * pc:(g + 1) * pc] = pd_g[tm - POOL_HALO:tm, :]
        off = SC_WIDTH + SSM_D_INNER + GM_WIDTH
        y_d = jnp.dot(pooled, pmap_ref[g], preferred_element_type=F32) * pscale_ref[:, g * pc:(g + 1) * pc]
        y_s[:, off + g * pc:off + (g + 1) * pc] = y_d.astype(BF16)

    bounds = (0, SC_WIDTH, SC_WIDTH + SSM_D_INNER, SC_WIDTH + SSM_D_INNER + GM_WIDTH, BR_TOTAL)
    merged = None
    for b, wbr_ref in enumerate((wbra_ref, wbrb_ref, wbrc_ref, wbrd_ref)):
        br = jnp.dot(y_s[:, bounds[b]:bounds[b + 1]], wbr_ref[...], preferred_element_type=F32)
        gate = jax.nn.sigmoid(jnp.dot(h_s[...], wgate_ref[:, b * D_MODEL:(b + 1) * D_MODEL],
                                      preferred_element_type=F32))
        term = gate * br
        merged = term if merged is None else merged + term
    o_ref[...] = x_ref[...] + jnp.dot(merged.astype(BF16), wout_ref[...], preferred_element_type=F32)


def _head_expand_matrix():
    r = np.zeros((LANES, 2 * SSM_D_INNER), np.float32)
    for piece in range(3):
        for q in range(2):
            for h in range(SSM_HEADS):
                rr = piece * 2 * SSM_HEADS + q * SSM_HEADS + h
                r[rr, q * SSM_D_INNER + h * SSM_HEAD_DIM:q * SSM_D_INNER + (h + 1) * SSM_HEAD_DIM] = 1.0
    return jnp.asarray(r, BF16)


def _mixer(x, p):
    L = x.shape[0]
    tm = TM_MIX
    full = lambda a: pl.BlockSpec(a.shape, lambda i, _n=a.ndim: (0,) * _n, pipeline_mode=pl.Buffered(1))
    consts = [p["norm1"], p["w_in"], p["w_in_dt"], p["w_in_tail"], p["w_gate"], p["scw"], p["mcw"], p["mcb"],
              p["dtb"], p["alog"], p["dskip"], p["ng"], p["lng"], p["lnb"], p["gws"], p["gbias"], p["pmap"],
              p["pscale"], p["rexp"], p["wbr_a"], p["wbr_b"], p["wbr_c"], p["wbr_d"], p["wout"]]
    est = (4 * tm * D_MODEL * 4 + sum(int(a.size) * a.dtype.itemsize for a in consts)
           + tm * (D_MODEL * 2 + P_TOTAL * 4 + LANES * 4 + SSM_CONV_DIM * 4 + BR_TOTAL * 2) + 12 * tm * D_MODEL * 4)
    return pl.pallas_call(
        _mixer_kernel,
        grid=(L // tm,),
        in_specs=[pl.BlockSpec((tm, D_MODEL), lambda i: (i, 0))] + [full(a) for a in consts],
        out_specs=pl.BlockSpec((tm, D_MODEL), lambda i: (i, 0)),
        out_shape=jax.ShapeDtypeStruct((L, D_MODEL), F32),
        scratch_shapes=[
            pltpu.VMEM((SUBLANES, SC_WIDTH), F32),
            pltpu.VMEM((SUBLANES, SSM_CONV_DIM), F32),
            pltpu.VMEM((POOL_HALO, POOL_WIDTH), F32),
            pltpu.VMEM((SSM_GROUPS * SSM_STATE, SSM_D_INNER // SSM_GROUPS), F32),
            pltpu.VMEM((tm, D_MODEL), BF16),
            pltpu.VMEM((tm, P_TOTAL), F32),
            pltpu.VMEM((tm, LANES), F32),
            pltpu.VMEM((tm, SSM_CONV_DIM), F32),
            pltpu.VMEM((tm, BR_TOTAL), BF16),
        ],
        compiler_params=pltpu.CompilerParams(
            dimension_semantics=("arbitrary",), vmem_limit_bytes=_vmem_limit(est)),
        name="mixers",
    )(x, *consts)


def _ffn_kernel(x_ref, g_ref, wg_ref, wu_ref, wd_ref, o_ref):
    x = x_ref[...]
    h = _rms(x, g_ref[...]).astype(BF16)
    acc = x
    for c in range(wg_ref.shape[1] // FFN_CHUNK):
        sl = slice(c * FFN_CHUNK, (c + 1) * FFN_CHUNK)
        gate = jnp.dot(h, wg_ref[:, sl], preferred_element_type=F32)
        up = jnp.dot(h, wu_ref[:, sl], preferred_element_type=F32)
        acc = acc + jnp.dot((_silu(gate) * up).astype(BF16), wd_ref[sl, :], preferred_element_type=F32)
    o_ref[...] = acc


def _dense_ffn(x, g, wg, wu, wd):
    L, d = x.shape
    f = wg.shape[1]
    tm = TM_FFN
    est = 4 * tm * d * 4 + 3 * d * f * 2 + 4 * tm * FFN_CHUNK * 4
    return pl.pallas_call(
        _ffn_kernel,
        grid=(L // tm,),
        in_specs=[pl.BlockSpec((tm, d), lambda i: (i, 0)),
                  pl.BlockSpec((1, d), lambda i: (0, 0)),
                  pl.BlockSpec((d, f), lambda i: (0, 0), pipeline_mode=pl.Buffered(1)),
                  pl.BlockSpec((d, f), lambda i: (0, 0), pipeline_mode=pl.Buffered(1)),
                  pl.BlockSpec((f, d), lambda i: (0, 0), pipeline_mode=pl.Buffered(1))],
        out_specs=pl.BlockSpec((tm, d), lambda i: (i, 0)),
        out_shape=jax.ShapeDtypeStruct((L, d), F32),
        compiler_params=pltpu.CompilerParams(
            dimension_semantics=("parallel",), vmem_limit_bytes=_vmem_limit(est)),
        name="dense_swiglu",
    )(x, g, wg, wu, wd)


def _router_kernel(x_ref, g_ref, rw_ref, ri_ref, rg_ref, cnt_ref, carry):
    i = pl.program_id(0)
    tm = x_ref.shape[0]

    @pl.when(i == 0)
    def _():
        carry[...] = jnp.zeros_like(carry)

    h = _rms(x_ref[...], g_ref[...])
    pieces = jnp.dot(jnp.concatenate(_split3(h), axis=0), rw_ref[...], preferred_element_type=F32)
    s3 = pieces[0:tm] + pieces[tm:2 * tm] + pieces[2 * tm:3 * tm]
    logits = s3 + pltpu.roll(s3, LANES - N_EXPERTS, 1) + pltpu.roll(s3, LANES - 2 * N_EXPERTS, 1)
    lane = lax.broadcasted_iota(jnp.int32, (1, LANES), 1).astype(F32)
    logits = jnp.where(lane < N_EXPERTS, logits, -jnp.inf)
    m1 = jnp.max(logits, axis=-1, keepdims=True)
    i1 = jnp.min(jnp.where(logits == m1, lane, float(LANES)), axis=-1, keepdims=True)
    rest = jnp.where(lane == i1, -jnp.inf, logits)
    m2 = jnp.max(rest, axis=-1, keepdims=True)
    i2 = jnp.min(jnp.where(rest == m2, lane, float(LANES)), axis=-1, keepdims=True)
    e21 = jnp.exp(m2 - m1)
    g1 = 1.0 / (1.0 + e21)
    g2 = e21 / (1.0 + e21)

    hot1 = lane == i1
    hot2 = lane == i2
    onehot = jnp.logical_or(hot1, hot2).astype(BF16)
    row = lax.broadcasted_iota(jnp.int32, (tm, 1), 0)
    colt = lax.broadcasted_iota(jnp.int32, (1, tm), 1)
    before = (row > colt).astype(BF16)
    seen = jnp.dot(before, onehot, preferred_element_type=F32) + carry[...]
    r1 = jnp.sum(jnp.where(hot1, seen, 0.0), axis=-1, keepdims=True)
    r2 = jnp.sum(jnp.where(hot2, seen, 0.0), axis=-1, keepdims=True)
    carry[...] = carry[...] + jnp.sum(onehot.astype(F32), axis=0, keepdims=True)

    ri = jnp.where(lane == 0, i1, jnp.where(lane == 1, i2, jnp.where(lane == 2, r1, r2)))
    ri_ref[...] = ri.astype(jnp.int32)
    rg_ref[...] = jnp.where(lane == 0, g1, g2)
    cnt_ref[...] = carry[...]


def _router(x, g, rw):
    L, d = x.shape
    tm = TM_ROUTER
    return pl.pallas_call(
        _router_kernel,
        grid=(L // tm,),
        in_specs=[pl.BlockSpec((tm, d), lambda i: (i, 0)),
                  pl.BlockSpec((1, d), lambda i: (0, 0)),
                  pl.BlockSpec((d, LANES), lambda i: (0, 0))],
        out_specs=[pl.BlockSpec((tm, LANES), lambda i: (i, 0)),
                   pl.BlockSpec((tm, LANES), lambda i: (i, 0)),
                   pl.BlockSpec((1, LANES), lambda i: (0, 0))],
        out_shape=[jax.ShapeDtypeStruct((L, LANES), jnp.int32),
                   jax.ShapeDtypeStruct((L, LANES), F32),
                   jax.ShapeDtypeStruct((1, LANES), F32)],
        scratch_shapes=[pltpu.VMEM((1, LANES), F32)],
        compiler_params=pltpu.CompilerParams(dimension_semantics=("arbitrary",)),
        name="router_top2",
    )(x, g, rw)


def _row_copy(src, dst, sem, s, d):
    return pltpu.make_async_copy(src.at[pl.ds(s, 1), :], dst.at[pl.ds(d, 1), :], sem)


def _start_row_copies(tm, copy_of):
    def start(t, c):
        for k in range(TOP_K):
            copy_of(t, k).start(priority=k)
        return c

    lax.fori_loop(0, tm, start, 0, unroll=DMA_UNROLL)


def _wait_row_copies(tm, copy_of):
    def wait(t, c):
        for k in range(TOP_K):
            copy_of(t, k).wait()
        return c

    lax.fori_loop(0, tm, wait, 0, unroll=DMA_UNROLL)


def _dispatch_kernel(pos_ref, pos_prev_ref, tail_ref, x_ref, g_ref, xs_ref, h_s, sems):
    j = pl.program_id(0)
    tm = x_ref.shape[0] // 2

    def copies(pref, half):
        return lambda t, k: _row_copy(h_s.at[half], xs_ref, sems.at[half], t,
                                      pref[0, 0, TOP_K * (half * tm + t) + k])

    @pl.when(j == 0)
    def _clear_unwritten_tiles():
        h_s[0] = jnp.zeros((tm, h_s.shape[2]), F32)
        n_tiles = xs_ref.shape[0] // TM_GROUP
        tiles = [(tail_ref[e] >= 0, tail_ref[e]) for e in range(N_EXPERTS)]
        tiles += [(t >= tail_ref[N_EXPERTS], t * TM_GROUP) for t in range(n_tiles - N_EXPERTS, n_tiles)]

        def clear_copy(first_row, part):
            first = first_row + part * tm
            if not isinstance(first, int):
                first = pl.multiple_of(first, tm)
            return pltpu.make_async_copy(h_s.at[0], xs_ref.at[pl.ds(first, tm), :], sems.at[0])

        for action in ("start", "wait"):
            for needed, first_row in tiles:
                @pl.when(needed)
                def _(first_row=first_row, action=action):
                    for part in range(TM_GROUP // tm):
                        getattr(clear_copy(first_row, part), action)()

    h_s[0] = _rms(x_ref[0:tm, :], g_ref[...])
    _start_row_copies(tm, copies(pos_ref, 0))

    @pl.when(j > 0)
    def _():
        _wait_row_copies(tm, copies(pos_prev_ref, 1))

    h_s[1] = _rms(x_ref[tm:2 * tm, :], g_ref[...])
    _start_row_copies(tm, copies(pos_ref, 1))
    _wait_row_copies(tm, copies(pos_ref, 0))

    @pl.when(j == pl.num_programs(0) - 1)
    def _():
        _wait_row_copies(tm, copies(pos_ref, 1))


def _dispatch(x, g, pos, tail_rows, n_sorted):
    L, d = x.shape
    tm = TM_ROUTE
    assert TM_GROUP % tm == 0 and L % (2 * tm) == 0
    pos3 = pos.reshape(L // (2 * tm), 1, 2 * TOP_K * tm)
    return pl.pallas_call(
        _dispatch_kernel,
        grid=(L // (2 * tm),),
        in_specs=[pl.BlockSpec((1, 1, 2 * TOP_K * tm), lambda j: (j, 0, 0), memory_space=pltpu.SMEM),
                  pl.BlockSpec((1, 1, 2 * TOP_K * tm), lambda j: (jnp.maximum(j - 1, 0), 0, 0),
                               memory_space=pltpu.SMEM),
                  pl.BlockSpec(memory_space=pltpu.SMEM),
                  pl.BlockSpec((2 * tm, d), lambda j: (j, 0)),
                  pl.BlockSpec((1, d), lambda j: (0, 0))],
        out_specs=pl.BlockSpec(memory_space=pl.ANY),
        out_shape=jax.ShapeDtypeStruct((n_sorted, d), F32),
        scratch_shapes=[pltpu.VMEM((2, tm, d), F32), pltpu.SemaphoreType.DMA((2,))],
        compiler_params=pltpu.CompilerParams(dimension_semantics=("arbitrary",)),
        name="moe_dispatch",
    )(pos3, pos3, tail_rows, x, g)


def _expert_kernel(te_ref, tv_ref, tb_ref, xs_ref, wg_ref, wu_ref, wd_ref, ys_ref):
    del tb_ref
    i = pl.program_id(0)

    @pl.when(tv_ref[i] == 0)
    def _():
        ys_ref[...] = jnp.zeros_like(ys_ref)

    @pl.when(tv_ref[i] != 0)
    def _():
        x = xs_ref[...].astype(BF16)
        acc = None
        for c in range(wg_ref.shape[2] // MOE_CHUNK):
            sl = slice(c * MOE_CHUNK, (c + 1) * MOE_CHUNK)
            gate = jnp.dot(x, wg_ref[0, :, sl], preferred_element_type=F32)
            up = jnp.dot(x, wu_ref[0, :, sl], preferred_element_type=F32)
            part = jnp.dot((_silu(gate) * up).astype(BF16), wd_ref[0, sl, :], preferred_element_type=F32)
            acc = part if acc is None else acc + part
        ys_ref[...] = acc


def _experts(xs, tile_expert, tile_valid, tile_block, wg, wu, wd):
    n_sorted, d = xs.shape
    f = wg.shape[2]
    tm = TM_GROUP
    single = pl.Buffered(1)
    est = 3 * d * f * 2 + 4 * tm * d * 4 + 4 * tm * MOE_CHUNK * 4
    return pl.pallas_call(
        _expert_kernel,
        grid_spec=pltpu.PrefetchScalarGridSpec(
            num_scalar_prefetch=3,
            grid=(n_sorted // tm,),
            in_specs=[pl.BlockSpec((tm, d), lambda i, te, tv, tb: (tb[i], 0)),
                      pl.BlockSpec((1, d, f), lambda i, te, tv, tb: (te[i], 0, 0), pipeline_mode=single),
                      pl.BlockSpec((1, d, f), lambda i, te, tv, tb: (te[i], 0, 0), pipeline_mode=single),
                      pl.BlockSpec((1, f, d), lambda i, te, tv, tb: (te[i], 0, 0), pipeline_mode=single)],
            out_specs=pl.BlockSpec((tm, d), lambda i, te, tv, tb: (i, 0)),
        ),
        out_shape=jax.ShapeDtypeStruct((n_sorted, d), F32),
        compiler_params=pltpu.CompilerParams(
            dimension_semantics=("arbitrary",), vmem_limit_bytes=_vmem_limit(est)),
        name="expert_swiglu",
    )(tile_expert, tile_valid, tile_block, xs, wg, wu, wd)


def _combine_kernel(pos_ref, pos_next_ref, x_ref, rg_ref, fg_ref, ys_ref, o_ref, buf, sems):
    j = pl.program_id(0)
    tm = x_ref.shape[0] // 2

    def copies(pref, half):
        return lambda t, k: _row_copy(ys_ref, buf.at[half, k], sems.at[half],
                                      pref[0, 0, TOP_K * (half * tm + t) + k], t)

    def combine(half):
        rows = slice(half * tm, (half + 1) * tm)
        rg = rg_ref[rows, :]
        y = x_ref[rows, :] + rg[:, 0:1] * buf[half, 0] + rg[:, 1:2] * buf[half, 1]
        o_ref[rows, :] = _rms(y, fg_ref[...])

    @pl.when(j == 0)
    def _():
        _start_row_copies(tm, copies(pos_ref, 0))

    _start_row_copies(tm, copies(pos_ref, 1))
    _wait_row_copies(tm, copies(pos_ref, 0))
    combine(0)

    @pl.when(j < pl.num_programs(0) - 1)
    def _():
        _start_row_copies(tm, copies(pos_next_ref, 0))

    _wait_row_copies(tm, copies(pos_ref, 1))
    combine(1)


def _combine(x, rg, pos, ys, final_g):
    L, d = x.shape
    tm = TM_ROUTE
    n = L // (2 * tm)
    pos3 = pos.reshape(n, 1, 2 * TOP_K * tm)
    return pl.pallas_call(
        _combine_kernel,
        grid=(n,),
        in_specs=[pl.BlockSpec((1, 1, 2 * TOP_K * tm), lambda j: (j, 0, 0), memory_space=pltpu.SMEM),
                  pl.BlockSpec((1, 1, 2 * TOP_K * tm), lambda j: (jnp.minimum(j + 1, n - 1), 0, 0),
                               memory_space=pltpu.SMEM),
                  pl.BlockSpec((2 * tm, d), lambda j: (j, 0)),
                  pl.BlockSpec((2 * tm, LANES), lambda j: (j, 0)),
                  pl.BlockSpec((1, d), lambda j: (0, 0)),
                  pl.BlockSpec(memory_space=pl.ANY)],
        out_specs=pl.BlockSpec((2 * tm, d), lambda j: (j, 0)),
        out_shape=jax.ShapeDtypeStruct((L, d), F32),
        scratch_shapes=[pltpu.VMEM((2, TOP_K, tm, d), F32), pltpu.SemaphoreType.DMA((2,))],
        compiler_params=pltpu.CompilerParams(dimension_semantics=("arbitrary",)),
        name="moe_combine_norm",
    )(pos3, pos3, x, rg, final_g, ys)


def _moe_ffn_final_norm(x, g2, router_w, wg, wu, wd, final_g):
    L, d = x.shape
    rw_hi = router_w.astype(BF16)
    rw_mid = (router_w - rw_hi.astype(F32)).astype(BF16)
    rw_lo = (router_w - rw_hi.astype(F32) - rw_mid.astype(F32)).astype(BF16)
    rw = jnp.pad(jnp.concatenate([rw_hi, rw_mid, rw_lo], axis=1), ((0, 0), (0, LANES - 3 * N_EXPERTS)))
    ri, rg, cnt = _router(x, g2, rw)
    counts = cnt[0, :N_EXPERTS].astype(jnp.int32)
    padded = ((counts + TM_GROUP - 1) // TM_GROUP) * TM_GROUP
    ends = jnp.cumsum(padded)
    starts = ends - padded
    pos = starts[ri[:, 0:TOP_K]] + ri[:, TOP_K:2 * TOP_K]
    n_sorted = TOP_K * L + N_EXPERTS * TM_GROUP
    tile_row = jnp.arange(n_sorted // TM_GROUP, dtype=jnp.int32) * TM_GROUP
    tile_valid = (tile_row < ends[-1]).astype(jnp.int32)
    last_row = jnp.minimum(tile_row, ends[-1] - TM_GROUP)
    tile_expert = jnp.sum((ends[None, :] <= last_row[:, None]).astype(jnp.int32), axis=1)
    tile_expert = jnp.minimum(tile_expert, N_EXPERTS - 1)
    tile_block = last_row // TM_GROUP
    tail_rows = jnp.concatenate(
        [jnp.where(padded > 0, ends - TM_GROUP, -1), ends[-1:] // TM_GROUP]).astype(jnp.int32)

    xs = _dispatch(x, g2, pos, tail_rows, n_sorted)
    ys = _experts(xs, tile_expert, tile_valid, tile_block, wg.astype(BF16), wu.astype(BF16), wd.astype(BF16))
    return _combine(x, rg, pos, ys, final_g.reshape(1, d))


def kernel(x, norm1_g, w_in, w_gate, sc_conv_w, ssm_conv_w, ssm_conv_b, ssm_dt_bias, ssm_a_log, ssm_d, ssm_norm_g, gm_ln_g, gm_ln_b, gm_ws, gm_bias, pool_map, pool_scale, w_br_a, w_br_b, w_br_c, w_br_d, w_out, norm2_g, ffn_wg, ffn_wu, ffn_wd, router_w, moe_wg, moe_wu, moe_wd, final_g):
    bsz, L, d = x.shape
    assert bsz == 1 and d == D_MODEL and L % TM_FFN == 0
    depth = norm1_g.shape[0]
    assert depth == 2, "layer 0 uses the dense SwiGLU, layer 1 the expert SwiGLU followed by the final norm"
    xt = x.reshape(L, d)
    rexp = _head_expand_matrix()
    pad_h = LANES - SSM_HEADS
    for layer in range(depth):
        w_in_l = w_in[layer]
        p = {
            "norm1": norm1_g[layer].reshape(1, d),
            "w_in": w_in_l[:, :P_DT].astype(BF16),
            "w_in_dt": jnp.pad(w_in_l[:, P_DT:IN_DT_END], ((0, 0), (0, pad_h))).astype(BF16),
            "w_in_tail": w_in_l[:, IN_DT_END:].astype(BF16),
            "w_gate": w_gate[layer].astype(BF16),
            "scw": sc_conv_w[layer],
            "mcw": ssm_conv_w[layer],
            "mcb": ssm_conv_b[layer].reshape(1, -1),
            "dtb": jnp.pad(ssm_dt_bias[layer], (0, pad_h)).reshape(1, LANES),
            "alog": jnp.pad(ssm_a_log[layer], (0, pad_h)).reshape(1, LANES),
            "dskip": jnp.repeat(ssm_d[layer], SSM_HEAD_DIM).reshape(1, SSM_D_INNER),
            "ng": ssm_norm_g[layer].reshape(1, -1),
            "lng": gm_ln_g[layer].reshape(1, -1),
            "lnb": gm_ln_b[layer].reshape(1, -1),
            "gws": gm_ws[layer],
            "gbias": jnp.repeat(gm_bias[layer].T, GM_WIDTH // GM_GROUPS, axis=1),
            "pmap": pool_map[layer].astype(BF16),
            "pscale": pool_scale[layer].reshape(1, -1),
            "rexp": rexp,
            "wbr_a": w_br_a[layer].astype(BF16),
            "wbr_b": w_br_b[layer].astype(BF16),
            "wbr_c": w_br_c[layer].astype(BF16),
            "wbr_d": w_br_d[layer].astype(BF16),
            "wout": w_out[layer].astype(BF16),
        }
        xt = _mixer(xt, p)
        idx = layer // 2
        g2 = norm2_g[layer].reshape(1, d)
        if layer % 2 == 0:
            xt = _dense_ffn(xt, g2, ffn_wg[idx].astype(BF16), ffn_wu[idx].astype(BF16), ffn_wd[idx].astype(BF16))
        else:
            xt = _moe_ffn_final_norm(xt, g2, router_w[idx], moe_wg[idx], moe_wu[idx], moe_wd[idx], final_g)
    return xt.reshape(bsz, L, d)
```

```python
import functools

import numpy as np
import jax
import jax.numpy as jnp
from jax import lax
from jax.experimental import pallas as pl
from jax.experimental.pallas import tpu as pltpu

F32 = jnp.float32
BF16 = jnp.bfloat16
EPS = 1e-6

LANES = 128
SUBLANES = 8

D_MODEL = 1024
SC_WIDTH = 512
SSM_D_INNER = 1024
SSM_HEAD_DIM = 64
SSM_HEADS = 16
SSM_GROUPS = 4
SSM_STATE = 128
SSM_CONV_DIM = 2048
GM_WIDTH = 512
GM_GROUPS = 4
GM_BLOCK = 128
POOL_WIDTH = 512
POOL_WINDOWS = (2, 4, 8, 16)
POOL_HALO = 16
N_BRANCH = 4
N_EXPERTS = 8
TOP_K = 2

P_AH, P_AB, P_AC = 0, 512, 1024
P_Z = 1536
P_XBC = 2560
P_DT = 4608
P_U = P_DT + LANES
P_V = P_U + GM_WIDTH
P_PD = P_V + GM_WIDTH
P_TOTAL = P_PD + POOL_WIDTH
IN_DT_END = 4624

BR_TOTAL = SC_WIDTH + SSM_D_INNER + GM_WIDTH + POOL_WIDTH

TM_MIX = 256
SSD_CHUNK = 256
PROJ_CHUNK = 1792
TM_FFN = 512
TM_ROUTER = 512
TM_ROUTE = 256
TM_GROUP = 512
FFN_CHUNK = 256
MOE_CHUNK = 256
DMA_UNROLL = 8
W_LOAD_CHUNK = 512


def _vmem_limit(nbytes):
    return int(min(nbytes + (8 << 20), 60 << 20))


def _rms(x, g):
    return x * lax.rsqrt(jnp.mean(x * x, axis=-1, keepdims=True) + EPS) * g


def _silu(x):
    return x * jax.nn.sigmoid(x)


def _shift_rows(ext, k, halo, tm):
    return pltpu.roll(ext, k, 0)[halo:halo + tm]


def _split3(x):
    hi = x.astype(BF16)
    r1 = x - hi.astype(F32)
    mid = r1.astype(BF16)
    lo = (r1 - mid.astype(F32)).astype(BF16)
    return hi, mid, lo


def _mixer_kernel(x_ref, n1_ref, win_ref, windt_ref, wintail_ref, wgate_ref, scw_ref, mcw_ref, mcb_ref,
                  dtb_ref, alog_ref, dskip_ref, ng_ref, lng_ref, lnb_ref, gws_ref, gbias_ref, pmap_ref,
                  pscale_ref, rexp_ref, wbra_ref, wbrb_ref, wbrc_ref, wbrd_ref, wout_ref, o_ref,
                  halo_a, halo_x, halo_p, state, h_s, proj_ref, dt_s, xbc_s, y_s):
    i = pl.program_id(0)
    tm = x_ref.shape[0]

    @pl.when(i == 0)
    def _init():
        halo_a[...] = jnp.zeros_like(halo_a)
        halo_x[...] = jnp.zeros_like(halo_x)
        halo_p[...] = jnp.zeros_like(halo_p)
        state[...] = jnp.zeros_like(state)

    h_s[...] = _rms(x_ref[...], n1_ref[...]).astype(BF16)
    for c0 in range(0, P_DT, PROJ_CHUNK):
        sl = slice(c0, min(c0 + PROJ_CHUNK, P_DT))
        proj_ref[:, sl] = jnp.dot(h_s[...], win_ref[:, sl], preferred_element_type=F32)
    dt_s[...] = jnp.dot(h_s[...], windt_ref[...], preferred_element_type=F32)
    proj_ref[:, P_U:P_TOTAL] = jnp.dot(h_s[...], wintail_ref[...], preferred_element_type=F32)

    def pj(c0, width, rows=slice(None)):
        return proj_ref[rows, c0:c0 + width]

    row = lax.broadcasted_iota(jnp.int32, (tm, 1), 0)
    lane = lax.broadcasted_iota(jnp.int32, (1, LANES), 1)

    ch = pj(P_AC, SC_WIDTH) * pj(P_AH, SC_WIDTH)
    ext = jnp.concatenate([halo_a[...], ch], axis=0)
    scw = scw_ref[...]
    conv_a = (ch * scw[2:3, :]
              + _shift_rows(ext, 1, SUBLANES, tm) * scw[1:2, :]
              + _shift_rows(ext, 2, SUBLANES, tm) * scw[0:1, :])
    halo_a[...] = ch[tm - SUBLANES:tm, :]
    y_s[:, 0:SC_WIDTH] = (pj(P_AB, SC_WIDTH) * conv_a).astype(BF16)

    cw = 512
    for c in range(SSM_CONV_DIM // cw):
        xc = pj(P_XBC + c * cw, cw)
        extx = jnp.concatenate([halo_x[:, c * cw:(c + 1) * cw], xc], axis=0)
        mcw = mcw_ref[:, c * cw:(c + 1) * cw]
        conv = (xc * mcw[3:4, :]
                + _shift_rows(extx, 1, SUBLANES, tm) * mcw[2:3, :]
                + _shift_rows(extx, 2, SUBLANES, tm) * mcw[1:2, :]
                + _shift_rows(extx, 3, SUBLANES, tm) * mcw[0:1, :]
                + mcb_ref[:, c * cw:(c + 1) * cw])
        halo_x[:, c * cw:(c + 1) * cw] = xc[tm - SUBLANES:tm, :]
        xbc_s[:, c * cw:(c + 1) * cw] = _silu(conv)

    tc = SSD_CHUNK
    row_c = lax.broadcasted_iota(jnp.int32, (tc, 1), 0)
    col_c = lax.broadcasted_iota(jnp.int32, (1, tc), 1)
    causal = row_c >= col_c
    head_lane = lane < SSM_HEADS
    neg_a = -jnp.exp(alog_ref[...])
    gw = SSM_D_INNER // SSM_GROUPS
    lo_half = lane < SSM_HEAD_DIM
    for sc in range(tm // tc):
        rows = slice(sc * tc, (sc + 1) * tc)
        dt_in = dt_s[rows, :] + dtb_ref[...]
        dt = jnp.maximum(dt_in, 0.0) + jnp.log1p(jnp.exp(-jnp.abs(dt_in)))
        da = dt * neg_a
        d_hi, d_mid, d_lo = _split3(jnp.where(head_lane, da, 0.0))
        d3 = (d_hi.astype(F32) + pltpu.roll(d_mid.astype(F32), SSM_HEADS, 1)
              + pltpu.roll(d_lo.astype(F32), 2 * SSM_HEADS, 1)).astype(BF16)
        c3 = jnp.dot(causal.astype(BF16), d3, preferred_element_type=F32)
        cs = c3 + pltpu.roll(c3, LANES - SSM_HEADS, 1) + pltpu.roll(c3, LANES - 2 * SSM_HEADS, 1)
        cs = jnp.where(head_lane, cs, 0.0)
        decay_in = jnp.exp(cs)
        to_end = jnp.exp(cs[tc - 1:tc, :] - cs) * dt
        packed = (jnp.where(head_lane, decay_in, 0.0)
                  + pltpu.roll(jnp.where(head_lane, to_end, 0.0), SSM_HEADS, 1))
        hi, mid, lo = _split3(packed)
        x3 = (hi.astype(F32) + pltpu.roll(mid.astype(F32), 2 * SSM_HEADS, 1)
              + pltpu.roll(lo.astype(F32), 4 * SSM_HEADS, 1)).astype(BF16)
        expanded = jnp.dot(x3, rexp_ref[...], preferred_element_type=F32)
        decay_cols = expanded[:, 0:SSM_D_INNER]
        toend_cols = expanded[:, SSM_D_INNER:2 * SSM_D_INNER]

        tpk = jnp.where(head_lane, cs, 0.0) + pltpu.roll(jnp.where(head_lane, dt, 0.0), SSM_HEADS, 1)
        tpk_t = tpk.T

        for g in range(SSM_GROUPS):
            b_g = xbc_s[rows, SSM_D_INNER + g * SSM_STATE:SSM_D_INNER + (g + 1) * SSM_STATE].astype(BF16)
            c_off = SSM_D_INNER + SSM_GROUPS * SSM_STATE
            c_g = xbc_s[rows, c_off + g * SSM_STATE:c_off + (g + 1) * SSM_STATE].astype(BF16)
            cb = lax.dot_general(c_g, b_g, (((1,), (1,)), ((), ())), preferred_element_type=F32)
            s_g = state[g * SSM_STATE:(g + 1) * SSM_STATE, :]
            y_off = (jnp.dot(c_g, s_g.astype(BF16), preferred_element_type=F32)
                     * decay_cols[:, g * gw:(g + 1) * gw])
            xs_g = xbc_s[rows, g * gw:(g + 1) * gw]
            pieces = []
            for jj in range(2):
                xs_t = xs_g[:, jj * LANES:(jj + 1) * LANES].astype(BF16)
                acc = None
                for kk in range(2):
                    h = g * 4 + jj * 2 + kk
                    seg = jnp.exp(jnp.where(causal, cs[:, h:h + 1] - tpk_t[h:h + 1, :], -jnp.inf))
                    scores = (cb * seg * tpk_t[SSM_HEADS + h:SSM_HEADS + h + 1, :]).astype(BF16)
                    keep = lo_half if kk == 0 else jnp.logical_not(lo_half)
                    part = jnp.dot(scores, jnp.where(keep, xs_t, jnp.zeros_like(xs_t)),
                                   preferred_element_type=F32)
                    acc = part if acc is None else acc + part
                pieces.append(acc)
            y_g = jnp.concatenate(pieces, axis=1) + y_off + xs_g * dskip_ref[:, g * gw:(g + 1) * gw]
            upd = jnp.dot(b_g.T, (toend_cols[:, g * gw:(g + 1) * gw] * xs_g).astype(BF16),
                          preferred_element_type=F32)
            state[g * SSM_STATE:(g + 1) * SSM_STATE, :] = (
                decay_cols[tc - 1:tc, g * gw:(g + 1) * gw] * s_g + upd)
            y_g = y_g * _silu(pj(P_Z + g * gw, gw, rows))
            y_g = y_g * lax.rsqrt(jnp.mean(y_g * y_g, axis=-1, keepdims=True) + EPS)
            y_s[rows, SC_WIDTH + g * gw:SC_WIDTH + (g + 1) * gw] = (
                y_g * ng_ref[:, g * gw:(g + 1) * gw]).astype(BF16)

    def gelu(t):
        return 0.5 * t * (1.0 + jnp.tanh(0.7978845608028654 * (t + 0.044715 * (t * t * t))))

    v = gelu(pj(P_V, GM_WIDTH))
    mu = jnp.mean(v, axis=-1, keepdims=True)
    vc = v - mu
    var = jnp.mean(vc * vc, axis=-1, keepdims=True)
    vf = (vc * lax.rsqrt(var + EPS) * lng_ref[...] + lnb_ref[...]).astype(BF16)
    r_b = lax.broadcasted_iota(jnp.int32, (GM_BLOCK, 1), 0)
    c_b = lax.broadcasted_iota(jnp.int32, (1, GM_BLOCK), 1)
    gc = GM_WIDTH // GM_GROUPS
    for g in range(GM_GROUPS):
        wsm = jnp.where(r_b >= c_b, gws_ref[g], 0.0).astype(BF16)
        blocks = [jnp.dot(wsm, vf[b * GM_BLOCK:(b + 1) * GM_BLOCK, g * gc:(g + 1) * gc],
                          preferred_element_type=F32) + gbias_ref[:, g * gc:(g + 1) * gc]
                  for b in range(tm // GM_BLOCK)]
        s_sp = jnp.concatenate(blocks, axis=0)
        u_g = gelu(pj(P_U + g * gc, gc))
        off = SC_WIDTH + SSM_D_INNER
        y_s[:, off + g * gc:off + (g + 1) * gc] = (u_g * s_sp).astype(BF16)

    t_glob = i * tm + row
    pc = POOL_WIDTH // len(POOL_WINDOWS)
    for g, w in enumerate(POOL_WINDOWS):
        pd_g = pj(P_PD + g * pc, pc)
        s = jnp.concatenate([halo_p[:, g * pc:(g + 1) * pc], pd_g], axis=0)
        k = 1
        while k < w:
            s = s + pltpu.roll(s, k, 0)
            k *= 2
        win = s[POOL_HALO:POOL_HALO + tm]
        inv_cnt = 1.0 / jnp.minimum(t_glob + 1, w).astype(F32)
        pooled = (win * inv_cnt - pd_g).astype(BF16)
        halo_p[:, g * pc:(g + 1) * pc] = pd_g[tm - POOL_HALO:tm, :]
        off = SC_WIDTH + SSM_D_INNER + GM_WIDTH
        y_d = jnp.dot(pooled, pmap_ref[g], preferred_element_type=F32) * pscale_ref[:, g * pc:(g + 1) * pc]
        y_s[:, off + g * pc:off + (g + 1) * pc] = y_d.astype(BF16)

    bounds = (0, SC_WIDTH, SC_WIDTH + SSM_D_INNER, SC_WIDTH + SSM_D_INNER + GM_WIDTH, BR_TOTAL)
    merged = None
    for b, wbr_ref in enumerate((wbra_ref, wbrb_ref, wbrc_ref, wbrd_ref)):
        br = jnp.dot(y_s[:, bounds[b]:bounds[b + 1]], wbr_ref[...], preferred_element_type=F32)
        gate = jax.nn.sigmoid(jnp.dot(h_s[...], wgate_ref[:, b * D_MODEL:(b + 1) * D_MODEL],
                                      preferred_element_type=F32))
        term = gate * br
        merged = term if merged is None else merged + term
    o_ref[...] = x_ref[...] + jnp.dot(merged.astype(BF16), wout_ref[...], preferred_element_type=F32)


def _head_expand_matrix():
    r = np.zeros((LANES, 2 * SSM_D_INNER), np.float32)
    for piece in range(3):
        for q in range(2):
            for h in range(SSM_HEADS):
                rr = piece * 2 * SSM_HEADS + q * SSM_HEADS + h
                r[rr, q * SSM_D_INNER + h * SSM_HEAD_DIM:q * SSM_D_INNER + (h + 1) * SSM_HEAD_DIM] = 1.0
    return jnp.asarray(r, BF16)


def _mixer(x, p):
    L = x.shape[0]
    tm = TM_MIX
    full = lambda a: pl.BlockSpec(a.shape, lambda i, _n=a.ndim: (0,) * _n, pipeline_mode=pl.Buffered(1))
    consts = [p["norm1"], p["w_in"], p["w_in_dt"], p["w_in_tail"], p["w_gate"], p["scw"], p["mcw"], p["mcb"],
              p["dtb"], p["alog"], p["dskip"], p["ng"], p["lng"], p["lnb"], p["gws"], p["gbias"], p["pmap"],
              p["pscale"], p["rexp"], p["wbr_a"], p["wbr_b"], p["wbr_c"], p["wbr_d"], p["wout"]]
    est = (4 * tm * D_MODEL * 4 + sum(int(a.size) * a.dtype.itemsize for a in consts)
           + tm * (D_MODEL * 2 + P_TOTAL * 4 + LANES * 4 + SSM_CONV_DIM * 4 + BR_TOTAL * 2) + 12 * tm * D_MODEL * 4)
    return pl.pallas_call(
        _mixer_kernel,
        grid=(L // tm,),
        in_specs=[pl.BlockSpec((tm, D_MODEL), lambda i: (i, 0))] + [full(a) for a in consts],
        out_specs=pl.BlockSpec((tm, D_MODEL), lambda i: (i, 0)),
        out_shape=jax.ShapeDtypeStruct((L, D_MODEL), F32),
        scratch_shapes=[
            pltpu.VMEM((SUBLANES, SC_WIDTH), F32),
            pltpu.VMEM((SUBLANES, SSM_CONV_DIM), F32),
            pltpu.VMEM((POOL_HALO, POOL_WIDTH), F32),
            pltpu.VMEM((SSM_GROUPS * SSM_STATE, SSM_D_INNER // SSM_GROUPS), F32),
            pltpu.VMEM((tm, D_MODEL), BF16),
            pltpu.VMEM((tm, P_TOTAL), F32),
            pltpu.VMEM((tm, LANES), F32),
            pltpu.VMEM((tm, SSM_CONV_DIM), F32),
            pltpu.VMEM((tm, BR_TOTAL), BF16),
        ],
        compiler_params=pltpu.CompilerParams(
            dimension_semantics=("arbitrary",), vmem_limit_bytes=_vmem_limit(est)),
        name="mixers",
    )(x, *consts)


def _ffn_kernel(x_ref, g_ref, wg_ref, wu_ref, wd_ref, o_ref):
    x = x_ref[...]
    h = _rms(x, g_ref[...]).astype(BF16)
    acc = x
    for c in range(wg_ref.shape[1] // FFN_CHUNK):
        sl = slice(c * FFN_CHUNK, (c + 1) * FFN_CHUNK)
        gate = jnp.dot(h, wg_ref[:, sl], preferred_element_type=F32)
        up = jnp.dot(h, wu_ref[:, sl], preferred_element_type=F32)
        acc = acc + jnp.dot((_silu(gate) * up).astype(BF16), wd_ref[sl, :], preferred_element_type=F32)
    o_ref[...] = acc


def _dense_ffn(x, g, wg, wu, wd):
    L, d = x.shape
    f = wg.shape[1]
    tm = TM_FFN
    est = 4 * tm * d * 4 + 3 * d * f * 2 + 4 * tm * FFN_CHUNK * 4
    return pl.pallas_call(
        _ffn_kernel,
        grid=(L // tm,),
        in_specs=[pl.BlockSpec((tm, d), lambda i: (i, 0)),
                  pl.BlockSpec((1, d), lambda i: (0, 0)),
                  pl.BlockSpec((d, f), lambda i: (0, 0), pipeline_mode=pl.Buffered(1)),
                  pl.BlockSpec((d, f), lambda i: (0, 0), pipeline_mode=pl.Buffered(1)),
                  pl.BlockSpec((f, d), lambda i: (0, 0), pipeline_mode=pl.Buffered(1))],
        out_specs=pl.BlockSpec((tm, d), lambda i: (i, 0)),
        out_shape=jax.ShapeDtypeStruct((L, d), F32),
        compiler_params=pltpu.CompilerParams(
            dimension_semantics=("parallel",), vmem_limit_bytes=_vmem_limit(est)),
        name="dense_swiglu",
    )(x, g, wg, wu, wd)


def _router_kernel(x_ref, g_ref, rw_ref, ri_ref, rg_ref, cnt_ref, carry):
    i = pl.program_id(0)
    tm = x_ref.shape[0]

    @pl.when(i == 0)
    def _():
        carry[...] = jnp.zeros_like(carry)

    h = _rms(x_ref[...], g_ref[...])
    pieces = jnp.dot(jnp.concatenate(_split3(h), axis=0), rw_ref[...], preferred_element_type=F32)
    s3 = pieces[0:tm] + pieces[tm:2 * tm] + pieces[2 * tm:3 * tm]
    logits = s3 + pltpu.roll(s3, LANES - N_EXPERTS, 1) + pltpu.roll(s3, LANES - 2 * N_EXPERTS, 1)
    lane = lax.broadcasted_iota(jnp.int32, (1, LANES), 1).astype(F32)
    logits = jnp.where(lane < N_EXPERTS, logits, -jnp.inf)
    m1 = jnp.max(logits, axis=-1, keepdims=True)
    i1 = jnp.min(jnp.where(logits == m1, lane, float(LANES)), axis=-1, keepdims=True)
    rest = jnp.where(lane == i1, -jnp.inf, logits)
    m2 = jnp.max(rest, axis=-1, keepdims=True)
    i2 = jnp.min(jnp.where(rest == m2, lane, float(LANES)), axis=-1, keepdims=True)
    e21 = jnp.exp(m2 - m1)
    g1 = 1.0 / (1.0 + e21)
    g2 = e21 / (1.0 + e21)

    hot1 = lane == i1
    hot2 = lane == i2
    onehot = jnp.logical_or(hot1, hot2).astype(BF16)
    row = lax.broadcasted_iota(jnp.int32, (tm, 1), 0)
    colt = lax.broadcasted_iota(jnp.int32, (1, tm), 1)
    before = (row > colt).astype(BF16)
    seen = jnp.dot(before, onehot, preferred_element_type=F32) + carry[...]
    r1 = jnp.sum(jnp.where(hot1, seen, 0.0), axis=-1, keepdims=True)
    r2 = jnp.sum(jnp.where(hot2, seen, 0.0), axis=-1, keepdims=True)
    carry[...] = carry[...] + jnp.sum(onehot.astype(F32), axis=0, keepdims=True)

    ri = jnp.where(lane == 0, i1, jnp.where(lane == 1, i2, jnp.where(lane == 2, r1, r2)))
    ri_ref[...] = ri.astype(jnp.int32)
    rg_ref[...] = jnp.where(lane == 0, g1, g2)
    cnt_ref[...] = carry[...]


def _router(x, g, rw):
    L, d = x.shape
    tm = TM_ROUTER
    return pl.pallas_call(
        _router_kernel,
        grid=(L // tm,),
        in_specs=[pl.BlockSpec((tm, d), lambda i: (i, 0)),
                  pl.BlockSpec((1, d), lambda i: (0, 0)),
                  pl.BlockSpec((d, LANES), lambda i: (0, 0))],
        out_specs=[pl.BlockSpec((tm, LANES), lambda i: (i, 0)),
                   pl.BlockSpec((tm, LANES), lambda i: (i, 0)),
                   pl.BlockSpec((1, LANES), lambda i: (0, 0))],
        out_shape=[jax.ShapeDtypeStruct((L, LANES), jnp.int32),
                   jax.ShapeDtypeStruct((L, LANES), F32),
                   jax.ShapeDtypeStruct((1, LANES), F32)],
        scratch_shapes=[pltpu.VMEM((1, LANES), F32)],
        compiler_params=pltpu.CompilerParams(dimension_semantics=("arbitrary",)),
        name="router_top2",
    )(x, g, rw)


def _row_copy(src, dst, sem, s, d):
    return pltpu.make_async_copy(src.at[pl.ds(s, 1), :], dst.at[pl.ds(d, 1), :], sem)


def _start_row_copies(tm, copy_of):
    def start(t, c):
        for k in range(TOP_K):
            copy_of(t, k).start(priority=k)
        return c

    lax.fori_loop(0, tm, start, 0, unroll=DMA_UNROLL)


def _wait_row_copies(tm, copy_of):
    def wait(t, c):
        for k in range(TOP_K):
            copy_of(t, k).wait()
        return c

    lax.fori_loop(0, tm, wait, 0, unroll=DMA_UNROLL)


def _dispatch_kernel(pos_ref, pos_prev_ref, tail_ref, x_ref, g_ref, xs_ref, h_s, sems):
    j = pl.program_id(0)
    tm = x_ref.shape[0] // 2

    def copies(pref, half):
        return lambda t, k: _row_copy(h_s.at[half], xs_ref, sems.at[half], t,
                                      pref[0, 0, TOP_K * (half * tm + t) + k])

    @pl.when(j == 0)
    def _clear_unwritten_tiles():
        h_s[0] = jnp.zeros((tm, h_s.shape[2]), F32)
        n_tiles = xs_ref.shape[0] // TM_GROUP
        tiles = [(tail_ref[e] >= 0, tail_ref[e]) for e in range(N_EXPERTS)]
        tiles += [(t >= tail_ref[N_EXPERTS], t * TM_GROUP) for t in range(n_tiles - N_EXPERTS, n_tiles)]

        def clear_copy(first_row, part):
            first = first_row + part * tm
            if not isinstance(first, int):
                first = pl.multiple_of(first, tm)
            return pltpu.make_async_copy(h_s.at[0], xs_ref.at[pl.ds(first, tm), :], sems.at[0])

        for action in ("start", "wait"):
            for needed, first_row in tiles:
                @pl.when(needed)
                def _(first_row=first_row, action=action):
                    for part in range(TM_GROUP // tm):
                        getattr(clear_copy(first_row, part), action)()

    h_s[0] = _rms(x_ref[0:tm, :], g_ref[...])
    _start_row_copies(tm, copies(pos_ref, 0))

    @pl.when(j > 0)
    def _():
        _wait_row_copies(tm, copies(pos_prev_ref, 1))

    h_s[1] = _rms(x_ref[tm:2 * tm, :], g_ref[...])
    _start_row_copies(tm, copies(pos_ref, 1))
    _wait_row_copies(tm, copies(pos_ref, 0))

    @pl.when(j == pl.num_programs(0) - 1)
    def _():
        _wait_row_copies(tm, copies(pos_ref, 1))


def _dispatch(x, g, pos, tail_rows, n_sorted):
    L, d = x.shape
    tm = TM_ROUTE
    assert TM_GROUP % tm == 0 and L % (2 * tm) == 0
    pos3 = pos.reshape(L // (2 * tm), 1, 2 * TOP_K * tm)
    return pl.pallas_call(
        _dispatch_kernel,
        grid=(L // (2 * tm),),
        in_specs=[pl.BlockSpec((1, 1, 2 * TOP_K * tm), lambda j: (j, 0, 0), memory_space=pltpu.SMEM),
                  pl.BlockSpec((1, 1, 2 * TOP_K * tm), lambda j: (jnp.maximum(j - 1, 0), 0, 0),
                               memory_space=pltpu.SMEM),
                  pl.BlockSpec(memory_space=pltpu.SMEM),
                  pl.BlockSpec((2 * tm, d), lambda j: (j, 0)),
                  pl.BlockSpec((1, d), lambda j: (0, 0))],
        out_specs=pl.BlockSpec(memory_space=pl.ANY),
        out_shape=jax.ShapeDtypeStruct((n_sorted, d), F32),
        scratch_shapes=[pltpu.VMEM((2, tm, d), F32), pltpu.SemaphoreType.DMA((2,))],
        compiler_params=pltpu.CompilerParams(dimension_semantics=("arbitrary",)),
        name="moe_dispatch",
    )(pos3, pos3, tail_rows, x, g)


def _expert_kernel(te_ref, tv_ref, tb_ref, xs_ref, wg_hbm, wu_hbm, wd_hbm, ys_ref,
                   wg_s, wu_s, wd_s, stage_cols, stage_rows, sems, *, moe_layer):
    del tb_ref
    i = pl.program_id(0)
    e = te_ref[i]
    first_tile_of_expert = jnp.logical_or(i == 0, e != te_ref[jnp.maximum(i - 1, 0)])

    @pl.when(jnp.logical_and(tv_ref[i] != 0, first_tile_of_expert))
    def _load_expert_weights():
        n = wg_s.shape[1] // W_LOAD_CHUNK
        steps = [(kind, c) for kind in ("gate", "up", "down") for c in range(n)]

        def copy(k):
            kind, c = steps[k]
            span = pl.ds(c * W_LOAD_CHUNK, W_LOAD_CHUNK)
            if kind == "down":
                return pltpu.make_async_copy(wd_hbm.at[moe_layer, e, span, :], stage_rows.at[k % 2], sems.at[k % 2])
            src = wg_hbm if kind == "gate" else wu_hbm
            return pltpu.make_async_copy(src.at[moe_layer, e, :, span], stage_cols.at[k % 2], sems.at[k % 2])

        copy(0).start()
        for k, (kind, c) in enumerate(steps):
            if k + 1 < len(steps):
                copy(k + 1).start()
            copy(k).wait()
            sl = slice(c * W_LOAD_CHUNK, (c + 1) * W_LOAD_CHUNK)
            if kind == "gate":
                wg_s[:, sl] = stage_cols[k % 2].astype(BF16)
            elif kind == "up":
                wu_s[:, sl] = stage_cols[k % 2].astype(BF16)
            else:
                wd_s[sl, :] = stage_rows[k % 2].astype(BF16)

    @pl.when(tv_ref[i] == 0)
    def _():
        ys_ref[...] = jnp.zeros_like(ys_ref)

    @pl.when(tv_ref[i] != 0)
    def _():
        x = xs_ref[...].astype(BF16)
        acc = None
        for c in range(wg_s.shape[1] // MOE_CHUNK):
            sl = slice(c * MOE_CHUNK, (c + 1) * MOE_CHUNK)
            gate = jnp.dot(x, wg_s[:, sl], preferred_element_type=F32)
            up = jnp.dot(x, wu_s[:, sl], preferred_element_type=F32)
            part = jnp.dot((_silu(gate) * up).astype(BF16), wd_s[sl, :], preferred_element_type=F32)
            acc = part if acc is None else acc + part
        ys_ref[...] = acc


def _experts(xs, tile_expert, tile_valid, tile_block, wg, wu, wd, moe_layer):
    n_sorted, d = xs.shape
    f = wg.shape[3]
    tm = TM_GROUP
    assert f % W_LOAD_CHUNK == 0 and f % MOE_CHUNK == 0
    est = 3 * d * f * 2 + 4 * W_LOAD_CHUNK * d * 4 + 4 * tm * d * 4 + 4 * tm * MOE_CHUNK * 4
    return pl.pallas_call(
        functools.partial(_expert_kernel, moe_layer=moe_layer),
        grid_spec=pltpu.PrefetchScalarGridSpec(
            num_scalar_prefetch=3,
            grid=(n_sorted // tm,),
            in_specs=[pl.BlockSpec((tm, d), lambda i, te, tv, tb: (tb[i], 0)),
                      pl.BlockSpec(memory_space=pl.ANY),
                      pl.BlockSpec(memory_space=pl.ANY),
                      pl.BlockSpec(memory_space=pl.ANY)],
            out_specs=pl.BlockSpec((tm, d), lambda i, te, tv, tb: (i, 0)),
            scratch_shapes=[pltpu.VMEM((d, f), BF16), pltpu.VMEM((d, f), BF16), pltpu.VMEM((f, d), BF16),
                            pltpu.VMEM((2, d, W_LOAD_CHUNK), F32), pltpu.VMEM((2, W_LOAD_CHUNK, d), F32),
                            pltpu.SemaphoreType.DMA((2,))],
        ),
        out_shape=jax.ShapeDtypeStruct((n_sorted, d), F32),
        compiler_params=pltpu.CompilerParams(
            dimension_semantics=("arbitrary",), vmem_limit_bytes=_vmem_limit(est)),
        name="expert_swiglu",
    )(tile_expert, tile_valid, tile_block, xs, wg, wu, wd)


def _combine_kernel(pos_ref, pos_next_ref, x_ref, rg_ref, fg_ref, ys_ref, o_ref, buf, sems):
    j = pl.program_id(0)
    tm = x_ref.shape[0] // 2

    def copies(pref, half):
        return lambda t, k: _row_copy(ys_ref, buf.at[half, k], sems.at[half],
                                      pref[0, 0, TOP_K * (half * tm + t) + k], t)

    def combine(half):
        rows = slice(half * tm, (half + 1) * tm)
        rg = rg_ref[rows, :]
        y = x_ref[rows, :] + rg[:, 0:1] * buf[half, 0] + rg[:, 1:2] * buf[half, 1]
        o_ref[rows, :] = _rms(y, fg_ref[...])

    @pl.when(j == 0)
    def _():
        _start_row_copies(tm, copies(pos_ref, 0))

    _start_row_copies(tm, copies(pos_ref, 1))
    _wait_row_copies(tm, copies(pos_ref, 0))
    combine(0)

    @pl.when(j < pl.num_programs(0) - 1)
    def _():
        _start_row_copies(tm, copies(pos_next_ref, 0))

    _wait_row_copies(tm, copies(pos_ref, 1))
    combine(1)


def _combine(x, rg, pos, ys, final_g):
    L, d = x.shape
    tm = TM_ROUTE
    n = L // (2 * tm)
    pos3 = pos.reshape(n, 1, 2 * TOP_K * tm)
    return pl.pallas_call(
        _combine_kernel,
        grid=(n,),
        in_specs=[pl.BlockSpec((1, 1, 2 * TOP_K * tm), lambda j: (j, 0, 0), memory_space=pltpu.SMEM),
                  pl.BlockSpec((1, 1, 2 * TOP_K * tm), lambda j: (jnp.minimum(j + 1, n - 1), 0, 0),
                               memory_space=pltpu.SMEM),
                  pl.BlockSpec((2 * tm, d), lambda j: (j, 0)),
                  pl.BlockSpec((2 * tm, LANES), lambda j: (j, 0)),
                  pl.BlockSpec((1, d), lambda j: (0, 0)),
                  pl.BlockSpec(memory_space=pl.ANY)],
        out_specs=pl.BlockSpec((2 * tm, d), lambda j: (j, 0)),
        out_shape=jax.ShapeDtypeStruct((L, d), F32),
        scratch_shapes=[pltpu.VMEM((2, TOP_K, tm, d), F32), pltpu.SemaphoreType.DMA((2,))],
        compiler_params=pltpu.CompilerParams(dimension_semantics=("arbitrary",)),
        name="moe_combine_norm",
    )(pos3, pos3, x, rg, final_g, ys)


def _moe_ffn_final_norm(x, g2, router_w, wg, wu, wd, moe_layer, final_g):
    L, d = x.shape
    rw_hi = router_w.astype(BF16)
    rw_mid = (router_w - rw_hi.astype(F32)).astype(BF16)
    rw_lo = (router_w - rw_hi.astype(F32) - rw_mid.astype(F32)).astype(BF16)
    rw = jnp.pad(jnp.concatenate([rw_hi, rw_mid, rw_lo], axis=1), ((0, 0), (0, LANES - 3 * N_EXPERTS)))
    ri, rg, cnt = _router(x, g2, rw)
    counts = cnt[0, :N_EXPERTS].astype(jnp.int32)
    padded = ((counts + TM_GROUP - 1) // TM_GROUP) * TM_GROUP
    ends = jnp.cumsum(padded)
    starts = ends - padded
    pos = starts[ri[:, 0:TOP_K]] + ri[:, TOP_K:2 * TOP_K]
    n_sorted = TOP_K * L + N_EXPERTS * TM_GROUP
    tile_row = jnp.arange(n_sorted // TM_GROUP, dtype=jnp.int32) * TM_GROUP
    tile_valid = (tile_row < ends[-1]).astype(jnp.int32)
    last_row = jnp.minimum(tile_row, ends[-1] - TM_GROUP)
    tile_expert = jnp.sum((ends[None, :] <= last_row[:, None]).astype(jnp.int32), axis=1)
    tile_expert = jnp.minimum(tile_expert, N_EXPERTS - 1)
    tile_block = last_row // TM_GROUP
    tail_rows = jnp.concatenate(
        [jnp.where(padded > 0, ends - TM_GROUP, -1), ends[-1:] // TM_GROUP]).astype(jnp.int32)

    xs = _dispatch(x, g2, pos, tail_rows, n_sorted)
    ys = _experts(xs, tile_expert, tile_valid, tile_block, wg, wu, wd, moe_layer)
    return _combine(x, rg, pos, ys, final_g.reshape(1, d))


def kernel(x, norm1_g, w_in, w_gate, sc_conv_w, ssm_conv_w, ssm_conv_b, ssm_dt_bias, ssm_a_log, ssm_d, ssm_norm_g, gm_ln_g, gm_ln_b, gm_ws, gm_bias, pool_map, pool_scale, w_br_a, w_br_b, w_br_c, w_br_d, w_out, norm2_g, ffn_wg, ffn_wu, ffn_wd, router_w, moe_wg, moe_wu, moe_wd, final_g):
    bsz, L, d = x.shape
    assert bsz == 1 and d == D_MODEL and L % TM_FFN == 0
    depth = norm1_g.shape[0]
    assert depth == 2, "layer 0 uses the dense SwiGLU, layer 1 the expert SwiGLU followed by the final norm"
    xt = x.reshape(L, d)
    rexp = _head_expand_matrix()
    pad_h = LANES - SSM_HEADS
    for layer in range(depth):
        w_in_l = w_in[layer]
        p = {
            "norm1": norm1_g[layer].reshape(1, d),
            "w_in": w_in_l[:, :P_DT].astype(BF16),
            "w_in_dt": jnp.pad(w_in_l[:, P_DT:IN_DT_END], ((0, 0), (0, pad_h))).astype(BF16),
            "w_in_tail": w_in_l[:, IN_DT_END:].astype(BF16),
            "w_gate": w_gate[layer].astype(BF16),
            "scw": sc_conv_w[layer],
            "mcw": ssm_conv_w[layer],
            "mcb": ssm_conv_b[layer].reshape(1, -1),
            "dtb": jnp.pad(ssm_dt_bias[layer], (0, pad_h)).reshape(1, LANES),
            "alog": jnp.pad(ssm_a_log[layer], (0, pad_h)).reshape(1, LANES),
            "dskip": jnp.repeat(ssm_d[layer], SSM_HEAD_DIM).reshape(1, SSM_D_INNER),
            "ng": ssm_norm_g[layer].reshape(1, -1),
            "lng": gm_ln_g[layer].reshape(1, -1),
            "lnb": gm_ln_b[layer].reshape(1, -1),
            "gws": gm_ws[layer],
            "gbias": jnp.repeat(gm_bias[layer].T, GM_WIDTH // GM_GROUPS, axis=1),
            "pmap": pool_map[layer].astype(BF16),
            "pscale": pool_scale[layer].reshape(1, -1),
            "rexp": rexp,
            "wbr_a": w_br_a[layer].astype(BF16),
            "wbr_b": w_br_b[layer].astype(BF16),
            "wbr_c": w_br_c[layer].astype(BF16),
            "wbr_d": w_br_d[layer].astype(BF16),
            "wout": w_out[layer].astype(BF16),
        }
        xt = _mixer(xt, p)
        idx = layer // 2
        g2 = norm2_g[layer].reshape(1, d)
        if layer % 2 == 0:
            xt = _dense_ffn(xt, g2, ffn_wg[idx].astype(BF16), ffn_wu[idx].astype(BF16), ffn_wd[idx].astype(BF16))
        else:
            xt = _moe_ffn_final_norm(xt, g2, router_w[idx], moe_wg, moe_wu, moe_wd, idx, final_g)
    return xt.reshape(bsz, L, d)
```

```python
import functools

import numpy as np
import jax
import jax.numpy as jnp
from jax import lax
from jax.experimental import pallas as pl
from jax.experimental.pallas import tpu as pltpu

F32 = jnp.float32
BF16 = jnp.bfloat16
EPS = 1e-6

LANES = 128
SUBLANES = 8

D_MODEL = 1024
SC_WIDTH = 512
SSM_D_INNER = 1024
SSM_HEAD_DIM = 64
SSM_HEADS = 16
SSM_GROUPS = 4
SSM_STATE = 128
SSM_CONV_DIM = 2048
GM_WIDTH = 512
GM_GROUPS = 4
GM_BLOCK = 128
POOL_WIDTH = 512
POOL_WINDOWS = (2, 4, 8, 16)
POOL_HALO = 16
N_BRANCH = 4
N_EXPERTS = 8
TOP_K = 2

P_AH, P_AB, P_AC = 0, 512, 1024
P_Z = 1536
P_XBC = 2560
P_DT = 4608
P_U = P_DT + LANES
P_V = P_U + GM_WIDTH
P_PD = P_V + GM_WIDTH
P_TOTAL = P_PD + POOL_WIDTH
IN_DT_END = 4624

BR_TOTAL = SC_WIDTH + SSM_D_INNER + GM_WIDTH + POOL_WIDTH

TM_MIX = 256
SSD_CHUNK = 256
PROJ_CHUNK = 1792
TM_FFN = 512
TM_ROUTER = 512
TM_ROUTE = 256
TM_GROUP = 512
FFN_CHUNK = 256
MOE_CHUNK = 256
DMA_UNROLL = 8
W_LOAD_ROWS_UP = 128
W_LOAD_ROWS_DOWN = 512
W_LOAD_SLOTS = 4


def _vmem_limit(nbytes):
    return int(min(nbytes + (8 << 20), 60 << 20))


def _rms(x, g):
    return x * lax.rsqrt(jnp.mean(x * x, axis=-1, keepdims=True) + EPS) * g


def _silu(x):
    return x * jax.nn.sigmoid(x)


def _shift_rows(ext, k, halo, tm):
    return pltpu.roll(ext, k, 0)[halo:halo + tm]


def _split3(x):
    hi = x.astype(BF16)
    r1 = x - hi.astype(F32)
    mid = r1.astype(BF16)
    lo = (r1 - mid.astype(F32)).astype(BF16)
    return hi, mid, lo


def _mixer_kernel(x_ref, n1_ref, win_ref, windt_ref, wintail_ref, wgate_ref, scw_ref, mcw_ref, mcb_ref,
                  dtb_ref, alog_ref, dskip_ref, ng_ref, lng_ref, lnb_ref, gws_ref, gbias_ref, pmap_ref,
                  pscale_ref, rexp_ref, wbra_ref, wbrb_ref, wbrc_ref, wbrd_ref, wout_ref, o_ref,
                  halo_a, halo_x, halo_p, state, h_s, proj_ref, dt_s, xbc_s, y_s):
    i = pl.program_id(0)
    tm = x_ref.shape[0]

    @pl.when(i == 0)
    def _init():
        halo_a[...] = jnp.zeros_like(halo_a)
        halo_x[...] = jnp.zeros_like(halo_x)
        halo_p[...] = jnp.zeros_like(halo_p)
        state[...] = jnp.zeros_like(state)

    h_s[...] = _rms(x_ref[...], n1_ref[...]).astype(BF16)
    for c0 in range(0, P_DT, PROJ_CHUNK):
        sl = slice(c0, min(c0 + PROJ_CHUNK, P_DT))
        proj_ref[:, sl] = jnp.dot(h_s[...], win_ref[:, sl], preferred_element_type=F32)
    dt_s[...] = jnp.dot(h_s[...], windt_ref[...], preferred_element_type=F32)
    proj_ref[:, P_U:P_TOTAL] = jnp.dot(h_s[...], wintail_ref[...], preferred_element_type=F32)

    def pj(c0, width, rows=slice(None)):
        return proj_ref[rows, c0:c0 + width]

    row = lax.broadcasted_iota(jnp.int32, (tm, 1), 0)
    lane = lax.broadcasted_iota(jnp.int32, (1, LANES), 1)

    ch = pj(P_AC, SC_WIDTH) * pj(P_AH, SC_WIDTH)
    ext = jnp.concatenate([halo_a[...], ch], axis=0)
    scw = scw_ref[...]
    conv_a = (ch * scw[2:3, :]
              + _shift_rows(ext, 1, SUBLANES, tm) * scw[1:2, :]
              + _shift_rows(ext, 2, SUBLANES, tm) * scw[0:1, :])
    halo_a[...] = ch[tm - SUBLANES:tm, :]
    y_s[:, 0:SC_WIDTH] = (pj(P_AB, SC_WIDTH) * conv_a).astype(BF16)

    cw = 512
    for c in range(SSM_CONV_DIM // cw):
        xc = pj(P_XBC + c * cw, cw)
        extx = jnp.concatenate([halo_x[:, c * cw:(c + 1) * cw], xc], axis=0)
        mcw = mcw_ref[:, c * cw:(c + 1) * cw]
        conv = (xc * mcw[3:4, :]
                + _shift_rows(extx, 1, SUBLANES, tm) * mcw[2:3, :]
                + _shift_rows(extx, 2, SUBLANES, tm) * mcw[1:2, :]
                + _shift_rows(extx, 3, SUBLANES, tm) * mcw[0:1, :]
                + mcb_ref[:, c * cw:(c + 1) * cw])
        halo_x[:, c * cw:(c + 1) * cw] = xc[tm - SUBLANES:tm, :]
        xbc_s[:, c * cw:(c + 1) * cw] = _silu(conv)

    tc = SSD_CHUNK
    row_c = lax.broadcasted_iota(jnp.int32, (tc, 1), 0)
    col_c = lax.broadcasted_iota(jnp.int32, (1, tc), 1)
    causal = row_c >= col_c
    head_lane = lane < SSM_HEADS
    neg_a = -jnp.exp(alog_ref[...])
    gw = SSM_D_INNER // SSM_GROUPS
    lo_half = lane < SSM_HEAD_DIM
    for sc in range(tm // tc):
        rows = slice(sc * tc, (sc + 1) * tc)
        dt_in = dt_s[rows, :] + dtb_ref[...]
        dt = jnp.maximum(dt_in, 0.0) + jnp.log1p(jnp.exp(-jnp.abs(dt_in)))
        da = dt * neg_a
        d_hi, d_mid, d_lo = _split3(jnp.where(head_lane, da, 0.0))
        d3 = (d_hi.astype(F32) + pltpu.roll(d_mid.astype(F32), SSM_HEADS, 1)
              + pltpu.roll(d_lo.astype(F32), 2 * SSM_HEADS, 1)).astype(BF16)
        c3 = jnp.dot(causal.astype(BF16), d3, preferred_element_type=F32)
        cs = c3 + pltpu.roll(c3, LANES - SSM_HEADS, 1) + pltpu.roll(c3, LANES - 2 * SSM_HEADS, 1)
        cs = jnp.where(head_lane, cs, 0.0)
        decay_in = jnp.exp(cs)
        to_end = jnp.exp(cs[tc - 1:tc, :] - cs) * dt
        packed = (jnp.where(head_lane, decay_in, 0.0)
                  + pltpu.roll(jnp.where(head_lane, to_end, 0.0), SSM_HEADS, 1))
        hi, mid, lo = _split3(packed)
        x3 = (hi.astype(F32) + pltpu.roll(mid.astype(F32), 2 * SSM_HEADS, 1)
              + pltpu.roll(lo.astype(F32), 4 * SSM_HEADS, 1)).astype(BF16)
        expanded = jnp.dot(x3, rexp_ref[...], preferred_element_type=F32)
        decay_cols = expanded[:, 0:SSM_D_INNER]
        toend_cols = expanded[:, SSM_D_INNER:2 * SSM_D_INNER]

        tpk = jnp.where(head_lane, cs, 0.0) + pltpu.roll(jnp.where(head_lane, dt, 0.0), SSM_HEADS, 1)
        tpk_t = tpk.T

        for g in range(SSM_GROUPS):
            b_g = xbc_s[rows, SSM_D_INNER + g * SSM_STATE:SSM_D_INNER + (g + 1) * SSM_STATE].astype(BF16)
            c_off = SSM_D_INNER + SSM_GROUPS * SSM_STATE
            c_g = xbc_s[rows, c_off + g * SSM_STATE:c_off + (g + 1) * SSM_STATE].astype(BF16)
            cb = lax.dot_general(c_g, b_g, (((1,), (1,)), ((), ())), preferred_element_type=F32)
            s_g = state[g * SSM_STATE:(g + 1) * SSM_STATE, :]
            y_off = (jnp.dot(c_g, s_g.astype(BF16), preferred_element_type=F32)
                     * decay_cols[:, g * gw:(g + 1) * gw])
            xs_g = xbc_s[rows, g * gw:(g + 1) * gw]
            pieces = []
            for jj in range(2):
                xs_t = xs_g[:, jj * LANES:(jj + 1) * LANES].astype(BF16)
                acc = None
                for kk in range(2):
                    h = g * 4 + jj * 2 + kk
                    seg = jnp.exp(jnp.where(causal, cs[:, h:h + 1] - tpk_t[h:h + 1, :], -jnp.inf))
                    scores = (cb * seg * tpk_t[SSM_HEADS + h:SSM_HEADS + h + 1, :]).astype(BF16)
                    keep = lo_half if kk == 0 else jnp.logical_not(lo_half)
                    part = jnp.dot(scores, jnp.where(keep, xs_t, jnp.zeros_like(xs_t)),
                                   preferred_element_type=F32)
                    acc = part if acc is None else acc + part
                pieces.append(acc)
            y_g = jnp.concatenate(pieces, axis=1) + y_off + xs_g * dskip_ref[:, g * gw:(g + 1) * gw]
            upd = jnp.dot(b_g.T, (toend_cols[:, g * gw:(g + 1) * gw] * xs_g).astype(BF16),
                          preferred_element_type=F32)
            state[g * SSM_STATE:(g + 1) * SSM_STATE, :] = (
                decay_cols[tc - 1:tc, g * gw:(g + 1) * gw] * s_g + upd)
            y_g = y_g * _silu(pj(P_Z + g * gw, gw, rows))
            y_g = y_g * lax.rsqrt(jnp.mean(y_g * y_g, axis=-1, keepdims=True) + EPS)
            y_s[rows, SC_WIDTH + g * gw:SC_WIDTH + (g + 1) * gw] = (
                y_g * ng_ref[:, g * gw:(g + 1) * gw]).astype(BF16)

    def gelu(t):
        return 0.5 * t * (1.0 + jnp.tanh(0.7978845608028654 * (t + 0.044715 * (t * t * t))))

    v = gelu(pj(P_V, GM_WIDTH))
    mu = jnp.mean(v, axis=-1, keepdims=True)
    vc = v - mu
    var = jnp.mean(vc * vc, axis=-1, keepdims=True)
    vf = (vc * lax.rsqrt(var + EPS) * lng_ref[...] + lnb_ref[...]).astype(BF16)
    r_b = lax.broadcasted_iota(jnp.int32, (GM_BLOCK, 1), 0)
    c_b = lax.broadcasted_iota(jnp.int32, (1, GM_BLOCK), 1)
    gc = GM_WIDTH // GM_GROUPS
    for g in range(GM_GROUPS):
        wsm = jnp.where(r_b >= c_b, gws_ref[g], 0.0).astype(BF16)
        blocks = [jnp.dot(wsm, vf[b * GM_BLOCK:(b + 1) * GM_BLOCK, g * gc:(g + 1) * gc],
                          preferred_element_type=F32) + gbias_ref[:, g * gc:(g + 1) * gc]
                  for b in range(tm // GM_BLOCK)]
        s_sp = jnp.concatenate(blocks, axis=0)
        u_g = gelu(pj(P_U + g * gc, gc))
        off = SC_WIDTH + SSM_D_INNER
        y_s[:, off + g * gc:off + (g + 1) * gc] = (u_g * s_sp).astype(BF16)

    t_glob = i * tm + row
    pc = POOL_WIDTH // len(POOL_WINDOWS)
    for g, w in enumerate(POOL_WINDOWS):
        pd_g = pj(P_PD + g * pc, pc)
        s = jnp.concatenate([halo_p[:, g * pc:(g + 1) * pc], pd_g], axis=0)
        k = 1
        while k < w:
            s = s + pltpu.roll(s, k, 0)
            k *= 2
        win = s[POOL_HALO:POOL_HALO + tm]
        inv_cnt = 1.0 / jnp.minimum(t_glob + 1, w).astype(F32)
        pooled = (win * inv_cnt - pd_g).astype(BF16)
        halo_p[:, g * pc:(g + 1) * pc] = pd_g[tm - POOL_HALO:tm, :]
        off = SC_WIDTH + SSM_D_INNER + GM_WIDTH
        y_d = jnp.dot(pooled, pmap_ref[g], preferred_element_type=F32) * pscale_ref[:, g * pc:(g + 1) * pc]
        y_s[:, off + g * pc:off + (g + 1) * pc] = y_d.astype(BF16)

    bounds = (0, SC_WIDTH, SC_WIDTH + SSM_D_INNER, SC_WIDTH + SSM_D_INNER + GM_WIDTH, BR_TOTAL)
    merged = None
    for b, wbr_ref in enumerate((wbra_ref, wbrb_ref, wbrc_ref, wbrd_ref)):
        br = jnp.dot(y_s[:, bounds[b]:bounds[b + 1]], wbr_ref[...], preferred_element_type=F32)
        gate = jax.nn.sigmoid(jnp.dot(h_s[...], wgate_ref[:, b * D_MODEL:(b + 1) * D_MODEL],
                                      preferred_element_type=F32))
        term = gate * br
        merged = term if merged is None else merged + term
    o_ref[...] = x_ref[...] + jnp.dot(merged.astype(BF16), wout_ref[...], preferred_element_type=F32)


def _head_expand_matrix():
    r = np.zeros((LANES, 2 * SSM_D_INNER), np.float32)
    for piece in range(3):
        for q in range(2):
            for h in range(SSM_HEADS):
                rr = piece * 2 * SSM_HEADS + q * SSM_HEADS + h
                r[rr, q * SSM_D_INNER + h * SSM_HEAD_DIM:q * SSM_D_INNER + (h + 1) * SSM_HEAD_DIM] = 1.0
    return jnp.asarray(r, BF16)


def _mixer(x, p):
    L = x.shape[0]
    tm = TM_MIX
    full = lambda a: pl.BlockSpec(a.shape, lambda i, _n=a.ndim: (0,) * _n, pipeline_mode=pl.Buffered(1))
    consts = [p["norm1"], p["w_in"], p["w_in_dt"], p["w_in_tail"], p["w_gate"], p["scw"], p["mcw"], p["mcb"],
              p["dtb"], p["alog"], p["dskip"], p["ng"], p["lng"], p["lnb"], p["gws"], p["gbias"], p["pmap"],
              p["pscale"], p["rexp"], p["wbr_a"], p["wbr_b"], p["wbr_c"], p["wbr_d"], p["wout"]]
    est = (4 * tm * D_MODEL * 4 + sum(int(a.size) * a.dtype.itemsize for a in consts)
           + tm * (D_MODEL * 2 + P_TOTAL * 4 + LANES * 4 + SSM_CONV_DIM * 4 + BR_TOTAL * 2) + 12 * tm * D_MODEL * 4)
    return pl.pallas_call(
        _mixer_kernel,
        grid=(L // tm,),
        in_specs=[pl.BlockSpec((tm, D_MODEL), lambda i: (i, 0))] + [full(a) for a in consts],
        out_specs=pl.BlockSpec((tm, D_MODEL), lambda i: (i, 0)),
        out_shape=jax.ShapeDtypeStruct((L, D_MODEL), F32),
        scratch_shapes=[
            pltpu.VMEM((SUBLANES, SC_WIDTH), F32),
            pltpu.VMEM((SUBLANES, SSM_CONV_DIM), F32),
            pltpu.VMEM((POOL_HALO, POOL_WIDTH), F32),
            pltpu.VMEM((SSM_GROUPS * SSM_STATE, SSM_D_INNER // SSM_GROUPS), F32),
            pltpu.VMEM((tm, D_MODEL), BF16),
            pltpu.VMEM((tm, P_TOTAL), F32),
            pltpu.VMEM((tm, LANES), F32),
            pltpu.VMEM((tm, SSM_CONV_DIM), F32),
            pltpu.VMEM((tm, BR_TOTAL), BF16),
        ],
        compiler_params=pltpu.CompilerParams(
            dimension_semantics=("arbitrary",), vmem_limit_bytes=_vmem_limit(est)),
        name="mixers",
    )(x, *consts)


def _ffn_kernel(x_ref, g_ref, wg_ref, wu_ref, wd_ref, o_ref):
    x = x_ref[...]
    h = _rms(x, g_ref[...]).astype(BF16)
    acc = x
    for c in range(wg_ref.shape[1] // FFN_CHUNK):
        sl = slice(c * FFN_CHUNK, (c + 1) * FFN_CHUNK)
        gate = jnp.dot(h, wg_ref[:, sl], preferred_element_type=F32)
        up = jnp.dot(h, wu_ref[:, sl], preferred_element_type=F32)
        acc = acc + jnp.dot((_silu(gate) * up).astype(BF16), wd_ref[sl, :], preferred_element_type=F32)
    o_ref[...] = acc


def _dense_ffn(x, g, wg, wu, wd):
    L, d = x.shape
    f = wg.shape[1]
    tm = TM_FFN
    est = 4 * tm * d * 4 + 3 * d * f * 2 + 4 * tm * FFN_CHUNK * 4
    return pl.pallas_call(
        _ffn_kernel,
        grid=(L // tm,),
        in_specs=[pl.BlockSpec((tm, d), lambda i: (i, 0)),
                  pl.BlockSpec((1, d), lambda i: (0, 0)),
                  pl.BlockSpec((d, f), lambda i: (0, 0), pipeline_mode=pl.Buffered(1)),
                  pl.BlockSpec((d, f), lambda i: (0, 0), pipeline_mode=pl.Buffered(1)),
                  pl.BlockSpec((f, d), lambda i: (0, 0), pipeline_mode=pl.Buffered(1))],
        out_specs=pl.BlockSpec((tm, d), lambda i: (i, 0)),
        out_shape=jax.ShapeDtypeStruct((L, d), F32),
        compiler_params=pltpu.CompilerParams(
            dimension_semantics=("parallel",), vmem_limit_bytes=_vmem_limit(est)),
        name="dense_swiglu",
    )(x, g, wg, wu, wd)


def _router_kernel(x_ref, g_ref, rw_ref, ri_ref, rg_ref, cnt_ref, carry):
    i = pl.program_id(0)
    tm = x_ref.shape[0]

    @pl.when(i == 0)
    def _():
        carry[...] = jnp.zeros_like(carry)

    h = _rms(x_ref[...], g_ref[...])
    pieces = jnp.dot(jnp.concatenate(_split3(h), axis=0), rw_ref[...], preferred_element_type=F32)
    s3 = pieces[0:tm] + pieces[tm:2 * tm] + pieces[2 * tm:3 * tm]
    logits = s3 + pltpu.roll(s3, LANES - N_EXPERTS, 1) + pltpu.roll(s3, LANES - 2 * N_EXPERTS, 1)
    lane = lax.broadcasted_iota(jnp.int32, (1, LANES), 1).astype(F32)
    logits = jnp.where(lane < N_EXPERTS, logits, -jnp.inf)
    m1 = jnp.max(logits, axis=-1, keepdims=True)
    i1 = jnp.min(jnp.where(logits == m1, lane, float(LANES)), axis=-1, keepdims=True)
    rest = jnp.where(lane == i1, -jnp.inf, logits)
    m2 = jnp.max(rest, axis=-1, keepdims=True)
    i2 = jnp.min(jnp.where(rest == m2, lane, float(LANES)), axis=-1, keepdims=True)
    e21 = jnp.exp(m2 - m1)
    g1 = 1.0 / (1.0 + e21)
    g2 = e21 / (1.0 + e21)

    hot1 = lane == i1
    hot2 = lane == i2
    onehot = jnp.logical_or(hot1, hot2).astype(BF16)
    row = lax.broadcasted_iota(jnp.int32, (tm, 1), 0)
    colt = lax.broadcasted_iota(jnp.int32, (1, tm), 1)
    before = (row > colt).astype(BF16)
    seen = jnp.dot(before, onehot, preferred_element_type=F32) + carry[...]
    r1 = jnp.sum(jnp.where(hot1, seen, 0.0), axis=-1, keepdims=True)
    r2 = jnp.sum(jnp.where(hot2, seen, 0.0), axis=-1, keepdims=True)
    carry[...] = carry[...] + jnp.sum(onehot.astype(F32), axis=0, keepdims=True)

    ri = jnp.where(lane == 0, i1, jnp.where(lane == 1, i2, jnp.where(lane == 2, r1, r2)))
    ri_ref[...] = ri.astype(jnp.int32)
    rg_ref[...] = jnp.where(lane == 0, g1, g2)
    cnt_ref[...] = carry[...]


def _router(x, g, rw):
    L, d = x.shape
    tm = TM_ROUTER
    return pl.pallas_call(
        _router_kernel,
        grid=(L // tm,),
        in_specs=[pl.BlockSpec((tm, d), lambda i: (i, 0)),
                  pl.BlockSpec((1, d), lambda i: (0, 0)),
                  pl.BlockSpec((d, LANES), lambda i: (0, 0))],
        out_specs=[pl.BlockSpec((tm, LANES), lambda i: (i, 0)),
                   pl.BlockSpec((tm, LANES), lambda i: (i, 0)),
                   pl.BlockSpec((1, LANES), lambda i: (0, 0))],
        out_shape=[jax.ShapeDtypeStruct((L, LANES), jnp.int32),
                   jax.ShapeDtypeStruct((L, LANES), F32),
                   jax.ShapeDtypeStruct((1, LANES), F32)],
        scratch_shapes=[pltpu.VMEM((1, LANES), F32)],
        compiler_params=pltpu.CompilerParams(dimension_semantics=("arbitrary",)),
        name="router_top2",
    )(x, g, rw)


def _row_copy(src, dst, sem, s, d):
    return pltpu.make_async_copy(src.at[pl.ds(s, 1), :], dst.at[pl.ds(d, 1), :], sem)


def _start_row_copies(tm, copy_of):
    def start(t, c):
        for k in range(TOP_K):
            copy_of(t, k).start(priority=k)
        return c

    lax.fori_loop(0, tm, start, 0, unroll=DMA_UNROLL)


def _wait_row_copies(tm, copy_of):
    def wait(t, c):
        for k in range(TOP_K):
            copy_of(t, k).wait()
        return c

    lax.fori_loop(0, tm, wait, 0, unroll=DMA_UNROLL)


def _dispatch_kernel(pos_ref, pos_prev_ref, tail_ref, x_ref, g_ref, xs_ref, h_s, sems):
    j = pl.program_id(0)
    tm = x_ref.shape[0] // 2

    def copies(pref, half):
        return lambda t, k: _row_copy(h_s.at[half], xs_ref, sems.at[half], t,
                                      pref[0, 0, TOP_K * (half * tm + t) + k])

    @pl.when(j == 0)
    def _clear_unwritten_tiles():
        h_s[0] = jnp.zeros((tm, h_s.shape[2]), F32)
        n_tiles = xs_ref.shape[0] // TM_GROUP
        tiles = [(tail_ref[e] >= 0, tail_ref[e]) for e in range(N_EXPERTS)]
        tiles += [(t >= tail_ref[N_EXPERTS], t * TM_GROUP) for t in range(n_tiles - N_EXPERTS, n_tiles)]

        def clear_copy(first_row, part):
            first = first_row + part * tm
            if not isinstance(first, int):
                first = pl.multiple_of(first, tm)
            return pltpu.make_async_copy(h_s.at[0], xs_ref.at[pl.ds(first, tm), :], sems.at[0])

        for action in ("start", "wait"):
            for needed, first_row in tiles:
                @pl.when(needed)
                def _(first_row=first_row, action=action):
                    for part in range(TM_GROUP // tm):
                        getattr(clear_copy(first_row, part), action)()

    h_s[0] = _rms(x_ref[0:tm, :], g_ref[...])
    _start_row_copies(tm, copies(pos_ref, 0))

    @pl.when(j > 0)
    def _():
        _wait_row_copies(tm, copies(pos_prev_ref, 1))

    h_s[1] = _rms(x_ref[tm:2 * tm, :], g_ref[...])
    _start_row_copies(tm, copies(pos_ref, 1))
    _wait_row_copies(tm, copies(pos_ref, 0))

    @pl.when(j == pl.num_programs(0) - 1)
    def _():
        _wait_row_copies(tm, copies(pos_ref, 1))


def _dispatch(x, g, pos, tail_rows, n_sorted):
    L, d = x.shape
    tm = TM_ROUTE
    assert TM_GROUP % tm == 0 and L % (2 * tm) == 0
    pos3 = pos.reshape(L // (2 * tm), 1, 2 * TOP_K * tm)
    return pl.pallas_call(
        _dispatch_kernel,
        grid=(L // (2 * tm),),
        in_specs=[pl.BlockSpec((1, 1, 2 * TOP_K * tm), lambda j: (j, 0, 0), memory_space=pltpu.SMEM),
                  pl.BlockSpec((1, 1, 2 * TOP_K * tm), lambda j: (jnp.maximum(j - 1, 0), 0, 0),
                               memory_space=pltpu.SMEM),
                  pl.BlockSpec(memory_space=pltpu.SMEM),
                  pl.BlockSpec((2 * tm, d), lambda j: (j, 0)),
                  pl.BlockSpec((1, d), lambda j: (0, 0))],
        out_specs=pl.BlockSpec(memory_space=pl.ANY),
        out_shape=jax.ShapeDtypeStruct((n_sorted, d), F32),
        scratch_shapes=[pltpu.VMEM((2, tm, d), F32), pltpu.SemaphoreType.DMA((2,))],
        compiler_params=pltpu.CompilerParams(dimension_semantics=("arbitrary",)),
        name="moe_dispatch",
    )(pos3, pos3, tail_rows, x, g)


def _expert_kernel(te_ref, tv_ref, tb_ref, xs_ref, wg_hbm, wu_hbm, wd_hbm, ys_ref,
                   wg_s, wu_s, wd_s, stage_up, stage_down, sems, *, moe_layer):
    del tb_ref
    i = pl.program_id(0)
    e = te_ref[i]
    first_tile_of_expert = jnp.logical_or(i == 0, e != te_ref[jnp.maximum(i - 1, 0)])

    @pl.when(jnp.logical_and(tv_ref[i] != 0, first_tile_of_expert))
    def _load_expert_weights():
        slots = stage_up.shape[0]
        rows_up, rows_down = stage_up.shape[1], stage_down.shape[1]
        steps = ([(wg_hbm, wg_s, stage_up, r) for r in range(0, wg_s.shape[0], rows_up)]
                 + [(wu_hbm, wu_s, stage_up, r) for r in range(0, wu_s.shape[0], rows_up)]
                 + [(wd_hbm, wd_s, stage_down, r) for r in range(0, wd_s.shape[0], rows_down)])

        def copy(k):
            src, _, stage, r = steps[k]
            return pltpu.make_async_copy(src.at[moe_layer, e, pl.ds(r, stage.shape[1]), :],
                                         stage.at[k % slots], sems.at[k % slots])

        for k in range(min(slots - 1, len(steps))):
            copy(k).start()
        for k, (_, dst, stage, r) in enumerate(steps):
            if k + slots - 1 < len(steps):
                copy(k + slots - 1).start()
            copy(k).wait()
            dst[r:r + stage.shape[1], :] = stage[k % slots].astype(BF16)

    @pl.when(tv_ref[i] == 0)
    def _():
        ys_ref[...] = jnp.zeros_like(ys_ref)

    @pl.when(tv_ref[i] != 0)
    def _():
        x = xs_ref[...].astype(BF16)
        acc = None
        for c in range(wg_s.shape[1] // MOE_CHUNK):
            sl = slice(c * MOE_CHUNK, (c + 1) * MOE_CHUNK)
            gate = jnp.dot(x, wg_s[:, sl], preferred_element_type=F32)
            up = jnp.dot(x, wu_s[:, sl], preferred_element_type=F32)
            part = jnp.dot((_silu(gate) * up).astype(BF16), wd_s[sl, :], preferred_element_type=F32)
            acc = part if acc is None else acc + part
        ys_ref[...] = acc


def _experts(xs, tile_expert, tile_valid, tile_block, wg, wu, wd, moe_layer):
    n_sorted, d = xs.shape
    f = wg.shape[3]
    tm = TM_GROUP
    assert d % W_LOAD_ROWS_UP == 0 and f % W_LOAD_ROWS_DOWN == 0 and f % MOE_CHUNK == 0
    est = (3 * d * f * 2 + W_LOAD_SLOTS * (W_LOAD_ROWS_UP * f + W_LOAD_ROWS_DOWN * d) * 4
           + 4 * tm * d * 4 + 4 * tm * MOE_CHUNK * 4)
    return pl.pallas_call(
        functools.partial(_expert_kernel, moe_layer=moe_layer),
        grid_spec=pltpu.PrefetchScalarGridSpec(
            num_scalar_prefetch=3,
            grid=(n_sorted // tm,),
            in_specs=[pl.BlockSpec((tm, d), lambda i, te, tv, tb: (tb[i], 0)),
                      pl.BlockSpec(memory_space=pl.ANY),
                      pl.BlockSpec(memory_space=pl.ANY),
                      pl.BlockSpec(memory_space=pl.ANY)],
            out_specs=pl.BlockSpec((tm, d), lambda i, te, tv, tb: (i, 0)),
            scratch_shapes=[pltpu.VMEM((d, f), BF16), pltpu.VMEM((d, f), BF16), pltpu.VMEM((f, d), BF16),
                            pltpu.VMEM((W_LOAD_SLOTS, W_LOAD_ROWS_UP, f), F32),
                            pltpu.VMEM((W_LOAD_SLOTS, W_LOAD_ROWS_DOWN, d), F32),
                            pltpu.SemaphoreType.DMA((W_LOAD_SLOTS,))],
        ),
        out_shape=jax.ShapeDtypeStruct((n_sorted, d), F32),
        compiler_params=pltpu.CompilerParams(
            dimension_semantics=("arbitrary",), vmem_limit_bytes=_vmem_limit(est)),
        name="expert_swiglu",
    )(tile_expert, tile_valid, tile_block, xs, wg, wu, wd)


def _combine_kernel(pos_ref, pos_next_ref, x_ref, rg_ref, fg_ref, ys_ref, o_ref, buf, sems):
    j = pl.program_id(0)
    tm = x_ref.shape[0] // 2

    def copies(pref, half):
        return lambda t, k: _row_copy(ys_ref, buf.at[half, k], sems.at[half],
                                      pref[0, 0, TOP_K * (half * tm + t) + k], t)

    def combine(half):
        rows = slice(half * tm, (half + 1) * tm)
        rg = rg_ref[rows, :]
        y = x_ref[rows, :] + rg[:, 0:1] * buf[half, 0] + rg[:, 1:2] * buf[half, 1]
        o_ref[rows, :] = _rms(y, fg_ref[...])

    @pl.when(j == 0)
    def _():
        _start_row_copies(tm, copies(pos_ref, 0))

    _start_row_copies(tm, copies(pos_ref, 1))
    _wait_row_copies(tm, copies(pos_ref, 0))
    combine(0)

    @pl.when(j < pl.num_programs(0) - 1)
    def _():
        _start_row_copies(tm, copies(pos_next_ref, 0))

    _wait_row_copies(tm, copies(pos_ref, 1))
    combine(1)


def _combine(x, rg, pos, ys, final_g):
    L, d = x.shape
    tm = TM_ROUTE
    n = L // (2 * tm)
    pos3 = pos.reshape(n, 1, 2 * TOP_K * tm)
    return pl.pallas_call(
        _combine_kernel,
        grid=(n,),
        in_specs=[pl.BlockSpec((1, 1, 2 * TOP_K * tm), lambda j: (j, 0, 0), memory_space=pltpu.SMEM),
                  pl.BlockSpec((1, 1, 2 * TOP_K * tm), lambda j: (jnp.minimum(j + 1, n - 1), 0, 0),
                               memory_space=pltpu.SMEM),
                  pl.BlockSpec((2 * tm, d), lambda j: (j, 0)),
                  pl.BlockSpec((2 * tm, LANES), lambda j: (j, 0)),
                  pl.BlockSpec((1, d), lambda j: (0, 0)),
                  pl.BlockSpec(memory_space=pl.ANY)],
        out_specs=pl.BlockSpec((2 * tm, d), lambda j: (j, 0)),
        out_shape=jax.ShapeDtypeStruct((L, d), F32),
        scratch_shapes=[pltpu.VMEM((2, TOP_K, tm, d), F32), pltpu.SemaphoreType.DMA((2,))],
        compiler_params=pltpu.CompilerParams(dimension_semantics=("arbitrary",)),
        name="moe_combine_norm",
    )(pos3, pos3, x, rg, final_g, ys)


def _moe_ffn_final_norm(x, g2, router_w, wg, wu, wd, moe_layer, final_g):
    L, d = x.shape
    rw_hi = router_w.astype(BF16)
    rw_mid = (router_w - rw_hi.astype(F32)).astype(BF16)
    rw_lo = (router_w - rw_hi.astype(F32) - rw_mid.astype(F32)).astype(BF16)
    rw = jnp.pad(jnp.concatenate([rw_hi, rw_mid, rw_lo], axis=1), ((0, 0), (0, LANES - 3 * N_EXPERTS)))
    ri, rg, cnt = _router(x, g2, rw)
    counts = cnt[0, :N_EXPERTS].astype(jnp.int32)
    padded = ((counts + TM_GROUP - 1) // TM_GROUP) * TM_GROUP
    ends = jnp.cumsum(padded)
    starts = ends - padded
    pos = starts[ri[:, 0:TOP_K]] + ri[:, TOP_K:2 * TOP_K]
    n_sorted = TOP_K * L + N_EXPERTS * TM_GROUP
    tile_row = jnp.arange(n_sorted // TM_GROUP, dtype=jnp.int32) * TM_GROUP
    tile_valid = (tile_row < ends[-1]).astype(jnp.int32)
    last_row = jnp.minimum(tile_row, ends[-1] - TM_GROUP)
    tile_expert = jnp.sum((ends[None, :] <= last_row[:, None]).astype(jnp.int32), axis=1)
    tile_expert = jnp.minimum(tile_expert, N_EXPERTS - 1)
    tile_block = last_row // TM_GROUP
    tail_rows = jnp.concatenate(
        [jnp.where(padded > 0, ends - TM_GROUP, -1), ends[-1:] // TM_GROUP]).astype(jnp.int32)

    xs = _dispatch(x, g2, pos, tail_rows, n_sorted)
    ys = _experts(xs, tile_expert, tile_valid, tile_block, wg, wu, wd, moe_layer)
    return _combine(x, rg, pos, ys, final_g.reshape(1, d))


def kernel(x, norm1_g, w_in, w_gate, sc_conv_w, ssm_conv_w, ssm_conv_b, ssm_dt_bias, ssm_a_log, ssm_d, ssm_norm_g, gm_ln_g, gm_ln_b, gm_ws, gm_bias, pool_map, pool_scale, w_br_a, w_br_b, w_br_c, w_br_d, w_out, norm2_g, ffn_wg, ffn_wu, ffn_wd, router_w, moe_wg, moe_wu, moe_wd, final_g):
    bsz, L, d = x.shape
    assert bsz == 1 and d == D_MODEL and L % TM_FFN == 0
    depth = norm1_g.shape[0]
    assert depth == 2, "layer 0 uses the dense SwiGLU, layer 1 the expert SwiGLU followed by the final norm"
    xt = x.reshape(L, d)
    rexp = _head_expand_matrix()
    pad_h = LANES - SSM_HEADS
    for layer in range(depth):
        w_in_l = w_in[layer]
        p = {
            "norm1": norm1_g[layer].reshape(1, d),
            "w_in": w_in_l[:, :P_DT].astype(BF16),
            "w_in_dt": jnp.pad(w_in_l[:, P_DT:IN_DT_END], ((0, 0), (0, pad_h))).astype(BF16),
            "w_in_tail": w_in_l[:, IN_DT_END:].astype(BF16),
            "w_gate": w_gate[layer].astype(BF16),
            "scw": sc_conv_w[layer],
            "mcw": ssm_conv_w[layer],
            "mcb": ssm_conv_b[layer].reshape(1, -1),
            "dtb": jnp.pad(ssm_dt_bias[layer], (0, pad_h)).reshape(1, LANES),
            "alog": jnp.pad(ssm_a_log[layer], (0, pad_h)).reshape(1, LANES),
            "dskip": jnp.repeat(ssm_d[layer], SSM_HEAD_DIM).reshape(1, SSM_D_INNER),
            "ng": ssm_norm_g[layer].reshape(1, -1),
            "lng": gm_ln_g[layer].reshape(1, -1),
            "lnb": gm_ln_b[layer].reshape(1, -1),
            "gws": gm_ws[layer],
            "gbias": jnp.repeat(gm_bias[layer].T, GM_WIDTH // GM_GROUPS, axis=1),
            "pmap": pool_map[layer].astype(BF16),
            "pscale": pool_scale[layer].reshape(1, -1),
            "rexp": rexp,
            "wbr_a": w_br_a[layer].astype(BF16),
            "wbr_b": w_br_b[layer].astype(BF16),
            "wbr_c": w_br_c[layer].astype(BF16),
            "wbr_d": w_br_d[layer].astype(BF16),
            "wout": w_out[layer].astype(BF16),
        }
        xt = _mixer(xt, p)
        idx = layer // 2
        g2 = norm2_g[layer].reshape(1, d)
        if layer % 2 == 0:
            xt = _dense_ffn(xt, g2, ffn_wg[idx].astype(BF16), ffn_wu[idx].astype(BF16), ffn_wd[idx].astype(BF16))
        else:
            xt = _moe_ffn_final_norm(xt, g2, router_w[idx], moe_wg, moe_wu, moe_wd, idx, final_g)
    return xt.reshape(bsz, L, d)
```

```python
import functools

import numpy as np
import jax
import jax.numpy as jnp
from jax import lax
from jax.experimental import pallas as pl
from jax.experimental.pallas import tpu as pltpu

F32 = jnp.float32
BF16 = jnp.bfloat16
EPS = 1e-6

LANES = 128
SUBLANES = 8

D_MODEL = 1024
SC_WIDTH = 512
SSM_D_INNER = 1024
SSM_HEAD_DIM = 64
SSM_HEADS = 16
SSM_GROUPS = 4
SSM_STATE = 128
SSM_CONV_DIM = 2048
GM_WIDTH = 512
GM_GROUPS = 4
GM_BLOCK = 128
POOL_WIDTH = 512
POOL_WINDOWS = (2, 4, 8, 16)
POOL_HALO = 16
N_BRANCH = 4
N_EXPERTS = 8
TOP_K = 2

P_AH, P_AB, P_AC = 0, 512, 1024
P_Z = 1536
P_XBC = 2560
P_DT = 4608
P_U = P_DT + LANES
P_V = P_U + GM_WIDTH
P_PD = P_V + GM_WIDTH
P_TOTAL = P_PD + POOL_WIDTH
IN_DT_END = 4624

BR_TOTAL = SC_WIDTH + SSM_D_INNER + GM_WIDTH + POOL_WIDTH

TM_MIX = 256
SSD_CHUNK = 256
PROJ_CHUNK = 1792
TM_FFN = 512
TM_ROUTER = 512
ROUTER_ROWS = 16
TM_ROUTE = 256
TM_GROUP = 512
FFN_CHUNK = 256
MOE_CHUNK = 256
DMA_UNROLL = 8
W_LOAD_ROWS_UP = 128
W_LOAD_ROWS_DOWN = 512
W_LOAD_SLOTS = 4


def _vmem_limit(nbytes):
    return int(min(nbytes + (8 << 20), 60 << 20))


def _rms(x, g):
    return x * lax.rsqrt(jnp.mean(x * x, axis=-1, keepdims=True) + EPS) * g


def _silu(x):
    return x * jax.nn.sigmoid(x)


def _shift_rows(ext, k, halo, tm):
    return pltpu.roll(ext, k, 0)[halo:halo + tm]


def _split3(x):
    hi = x.astype(BF16)
    r1 = x - hi.astype(F32)
    mid = r1.astype(BF16)
    lo = (r1 - mid.astype(F32)).astype(BF16)
    return hi, mid, lo


def _mixer_kernel(x_ref, n1_ref, win_ref, windt_ref, wintail_ref, wgate_ref, scw_ref, mcw_ref, mcb_ref,
                  dtb_ref, alog_ref, dskip_ref, ng_ref, lng_ref, lnb_ref, gws_ref, gbias_ref, pmap_ref,
                  pscale_ref, rexp_ref, wbra_ref, wbrb_ref, wbrc_ref, wbrd_ref, wout_ref, o_ref,
                  halo_a, halo_x, halo_p, state, h_s, proj_ref, dt_s, xbc_s, y_s):
    i = pl.program_id(0)
    tm = x_ref.shape[0]

    @pl.when(i == 0)
    def _init():
        halo_a[...] = jnp.zeros_like(halo_a)
        halo_x[...] = jnp.zeros_like(halo_x)
        halo_p[...] = jnp.zeros_like(halo_p)
        state[...] = jnp.zeros_like(state)

    h_s[...] = _rms(x_ref[...], n1_ref[...]).astype(BF16)
    for c0 in range(0, P_DT, PROJ_CHUNK):
        sl = slice(c0, min(c0 + PROJ_CHUNK, P_DT))
        proj_ref[:, sl] = jnp.dot(h_s[...], win_ref[:, sl], preferred_element_type=F32)
    dt_s[...] = jnp.dot(h_s[...], windt_ref[...], preferred_element_type=F32)
    proj_ref[:, P_U:P_TOTAL] = jnp.dot(h_s[...], wintail_ref[...], preferred_element_type=F32)

    def pj(c0, width, rows=slice(None)):
        return proj_ref[rows, c0:c0 + width]

    row = lax.broadcasted_iota(jnp.int32, (tm, 1), 0)
    lane = lax.broadcasted_iota(jnp.int32, (1, LANES), 1)

    ch = pj(P_AC, SC_WIDTH) * pj(P_AH, SC_WIDTH)
    ext = jnp.concatenate([halo_a[...], ch], axis=0)
    scw = scw_ref[...]
    conv_a = (ch * scw[2:3, :]
              + _shift_rows(ext, 1, SUBLANES, tm) * scw[1:2, :]
              + _shift_rows(ext, 2, SUBLANES, tm) * scw[0:1, :])
    halo_a[...] = ch[tm - SUBLANES:tm, :]
    y_s[:, 0:SC_WIDTH] = (pj(P_AB, SC_WIDTH) * conv_a).astype(BF16)

    cw = 512
    for c in range(SSM_CONV_DIM // cw):
        xc = pj(P_XBC + c * cw, cw)
        extx = jnp.concatenate([halo_x[:, c * cw:(c + 1) * cw], xc], axis=0)
        mcw = mcw_ref[:, c * cw:(c + 1) * cw]
        conv = (xc * mcw[3:4, :]
                + _shift_rows(extx, 1, SUBLANES, tm) * mcw[2:3, :]
                + _shift_rows(extx, 2, SUBLANES, tm) * mcw[1:2, :]
                + _shift_rows(extx, 3, SUBLANES, tm) * mcw[0:1, :]
                + mcb_ref[:, c * cw:(c + 1) * cw])
        halo_x[:, c * cw:(c + 1) * cw] = xc[tm - SUBLANES:tm, :]
        xbc_s[:, c * cw:(c + 1) * cw] = _silu(conv)

    tc = SSD_CHUNK
    row_c = lax.broadcasted_iota(jnp.int32, (tc, 1), 0)
    col_c = lax.broadcasted_iota(jnp.int32, (1, tc), 1)
    causal = row_c >= col_c
    head_lane = lane < SSM_HEADS
    neg_a = -jnp.exp(alog_ref[...])
    gw = SSM_D_INNER // SSM_GROUPS
    lo_half = lane < SSM_HEAD_DIM
    for sc in range(tm // tc):
        rows = slice(sc * tc, (sc + 1) * tc)
        dt_in = dt_s[rows, :] + dtb_ref[...]
        dt = jnp.maximum(dt_in, 0.0) + jnp.log1p(jnp.exp(-jnp.abs(dt_in)))
        da = dt * neg_a
        d_hi, d_mid, d_lo = _split3(jnp.where(head_lane, da, 0.0))
        d3 = (d_hi.astype(F32) + pltpu.roll(d_mid.astype(F32), SSM_HEADS, 1)
              + pltpu.roll(d_lo.astype(F32), 2 * SSM_HEADS, 1)).astype(BF16)
        c3 = jnp.dot(causal.astype(BF16), d3, preferred_element_type=F32)
        cs = c3 + pltpu.roll(c3, LANES - SSM_HEADS, 1) + pltpu.roll(c3, LANES - 2 * SSM_HEADS, 1)
        cs = jnp.where(head_lane, cs, 0.0)
        decay_in = jnp.exp(cs)
        to_end = jnp.exp(cs[tc - 1:tc, :] - cs) * dt
        packed = (jnp.where(head_lane, decay_in, 0.0)
                  + pltpu.roll(jnp.where(head_lane, to_end, 0.0), SSM_HEADS, 1))
        hi, mid, lo = _split3(packed)
        x3 = (hi.astype(F32) + pltpu.roll(mid.astype(F32), 2 * SSM_HEADS, 1)
              + pltpu.roll(lo.astype(F32), 4 * SSM_HEADS, 1)).astype(BF16)
        expanded = jnp.dot(x3, rexp_ref[...], preferred_element_type=F32)
        decay_cols = expanded[:, 0:SSM_D_INNER]
        toend_cols = expanded[:, SSM_D_INNER:2 * SSM_D_INNER]

        tpk = jnp.where(head_lane, cs, 0.0) + pltpu.roll(jnp.where(head_lane, dt, 0.0), SSM_HEADS, 1)
        tpk_t = tpk.T

        for g in range(SSM_GROUPS):
            b_g = xbc_s[rows, SSM_D_INNER + g * SSM_STATE:SSM_D_INNER + (g + 1) * SSM_STATE].astype(BF16)
            c_off = SSM_D_INNER + SSM_GROUPS * SSM_STATE
            c_g = xbc_s[rows, c_off + g * SSM_STATE:c_off + (g + 1) * SSM_STATE].astype(BF16)
            cb = lax.dot_general(c_g, b_g, (((1,), (1,)), ((), ())), preferred_element_type=F32)
            s_g = state[g * SSM_STATE:(g + 1) * SSM_STATE, :]
            y_off = (jnp.dot(c_g, s_g.astype(BF16), preferred_element_type=F32)
                     * decay_cols[:, g * gw:(g + 1) * gw])
            xs_g = xbc_s[rows, g * gw:(g + 1) * gw]
            pieces = []
            for jj in range(2):
                xs_t = xs_g[:, jj * LANES:(jj + 1) * LANES].astype(BF16)
                acc = None
                for kk in range(2):
                    h = g * 4 + jj * 2 + kk
                    seg = jnp.exp(jnp.where(causal, cs[:, h:h + 1] - tpk_t[h:h + 1, :], -jnp.inf))
                    scores = (cb * seg * tpk_t[SSM_HEADS + h:SSM_HEADS + h + 1, :]).astype(BF16)
                    keep = lo_half if kk == 0 else jnp.logical_not(lo_half)
                    part = jnp.dot(scores, jnp.where(keep, xs_t, jnp.zeros_like(xs_t)),
                                   preferred_element_type=F32)
                    acc = part if acc is None else acc + part
                pieces.append(acc)
            y_g = jnp.concatenate(pieces, axis=1) + y_off + xs_g * dskip_ref[:, g * gw:(g + 1) * gw]
            upd = jnp.dot(b_g.T, (toend_cols[:, g * gw:(g + 1) * gw] * xs_g).astype(BF16),
                          preferred_element_type=F32)
            state[g * SSM_STATE:(g + 1) * SSM_STATE, :] = (
                decay_cols[tc - 1:tc, g * gw:(g + 1) * gw] * s_g + upd)
            y_g = y_g * _silu(pj(P_Z + g * gw, gw, rows))
            y_g = y_g * lax.rsqrt(jnp.mean(y_g * y_g, axis=-1, keepdims=True) + EPS)
            y_s[rows, SC_WIDTH + g * gw:SC_WIDTH + (g + 1) * gw] = (
                y_g * ng_ref[:, g * gw:(g + 1) * gw]).astype(BF16)

    def gelu(t):
        return 0.5 * t * (1.0 + jnp.tanh(0.7978845608028654 * (t + 0.044715 * (t * t * t))))

    v = gelu(pj(P_V, GM_WIDTH))
    mu = jnp.mean(v, axis=-1, keepdims=True)
    vc = v - mu
    var = jnp.mean(vc * vc, axis=-1, keepdims=True)
    vf = (vc * lax.rsqrt(var + EPS) * lng_ref[...] + lnb_ref[...]).astype(BF16)
    r_b = lax.broadcasted_iota(jnp.int32, (GM_BLOCK, 1), 0)
    c_b = lax.broadcasted_iota(jnp.int32, (1, GM_BLOCK), 1)
    gc = GM_WIDTH // GM_GROUPS
    for g in range(GM_GROUPS):
        wsm = jnp.where(r_b >= c_b, gws_ref[g], 0.0).astype(BF16)
        blocks = [jnp.dot(wsm, vf[b * GM_BLOCK:(b + 1) * GM_BLOCK, g * gc:(g + 1) * gc],
                          preferred_element_type=F32) + gbias_ref[:, g * gc:(g + 1) * gc]
                  for b in range(tm // GM_BLOCK)]
        s_sp = jnp.concatenate(blocks, axis=0)
        u_g = gelu(pj(P_U + g * gc, gc))
        off = SC_WIDTH + SSM_D_INNER
        y_s[:, off + g * gc:off + (g + 1) * gc] = (u_g * s_sp).astype(BF16)

    t_glob = i * tm + row
    pc = POOL_WIDTH // len(POOL_WINDOWS)
    for g, w in enumerate(POOL_WINDOWS):
        pd_g = pj(P_PD + g * pc, pc)
        s = jnp.concatenate([halo_p[:, g * pc:(g + 1) * pc], pd_g], axis=0)
        k = 1
        while k < w:
            s = s + pltpu.roll(s, k, 0)
            k *= 2
        win = s[POOL_HALO:POOL_HALO + tm]
        inv_cnt = 1.0 / jnp.minimum(t_glob + 1, w).astype(F32)
        pooled = (win * inv_cnt - pd_g).astype(BF16)
        halo_p[:, g * pc:(g + 1) * pc] = pd_g[tm - POOL_HALO:tm, :]
        off = SC_WIDTH + SSM_D_INNER + GM_WIDTH
        y_d = jnp.dot(pooled, pmap_ref[g], preferred_element_type=F32) * pscale_ref[:, g * pc:(g + 1) * pc]
        y_s[:, off + g * pc:off + (g + 1) * pc] = y_d.astype(BF16)

    bounds = (0, SC_WIDTH, SC_WIDTH + SSM_D_INNER, SC_WIDTH + SSM_D_INNER + GM_WIDTH, BR_TOTAL)
    merged = None
    for b, wbr_ref in enumerate((wbra_ref, wbrb_ref, wbrc_ref, wbrd_ref)):
        br = jnp.dot(y_s[:, bounds[b]:bounds[b + 1]], wbr_ref[...], preferred_element_type=F32)
        gate = jax.nn.sigmoid(jnp.dot(h_s[...], wgate_ref[:, b * D_MODEL:(b + 1) * D_MODEL],
                                      preferred_element_type=F32))
        term = gate * br
        merged = term if merged is None else merged + term
    o_ref[...] = x_ref[...] + jnp.dot(merged.astype(BF16), wout_ref[...], preferred_element_type=F32)


def _head_expand_matrix():
    r = np.zeros((LANES, 2 * SSM_D_INNER), np.float32)
    for piece in range(3):
        for q in range(2):
            for h in range(SSM_HEADS):
                rr = piece * 2 * SSM_HEADS + q * SSM_HEADS + h
                r[rr, q * SSM_D_INNER + h * SSM_HEAD_DIM:q * SSM_D_INNER + (h + 1) * SSM_HEAD_DIM] = 1.0
    return jnp.asarray(r, BF16)


def _mixer(x, p):
    L = x.shape[0]
    tm = TM_MIX
    full = lambda a: pl.BlockSpec(a.shape, lambda i, _n=a.ndim: (0,) * _n, pipeline_mode=pl.Buffered(1))
    consts = [p["norm1"], p["w_in"], p["w_in_dt"], p["w_in_tail"], p["w_gate"], p["scw"], p["mcw"], p["mcb"],
              p["dtb"], p["alog"], p["dskip"], p["ng"], p["lng"], p["lnb"], p["gws"], p["gbias"], p["pmap"],
              p["pscale"], p["rexp"], p["wbr_a"], p["wbr_b"], p["wbr_c"], p["wbr_d"], p["wout"]]
    est = (4 * tm * D_MODEL * 4 + sum(int(a.size) * a.dtype.itemsize for a in consts)
           + tm * (D_MODEL * 2 + P_TOTAL * 4 + LANES * 4 + SSM_CONV_DIM * 4 + BR_TOTAL * 2) + 12 * tm * D_MODEL * 4)
    return pl.pallas_call(
        _mixer_kernel,
        grid=(L // tm,),
        in_specs=[pl.BlockSpec((tm, D_MODEL), lambda i: (i, 0))] + [full(a) for a in consts],
        out_specs=pl.BlockSpec((tm, D_MODEL), lambda i: (i, 0)),
        out_shape=jax.ShapeDtypeStruct((L, D_MODEL), F32),
        scratch_shapes=[
            pltpu.VMEM((SUBLANES, SC_WIDTH), F32),
            pltpu.VMEM((SUBLANES, SSM_CONV_DIM), F32),
            pltpu.VMEM((POOL_HALO, POOL_WIDTH), F32),
            pltpu.VMEM((SSM_GROUPS * SSM_STATE, SSM_D_INNER // SSM_GROUPS), F32),
            pltpu.VMEM((tm, D_MODEL), BF16),
            pltpu.VMEM((tm, P_TOTAL), F32),
            pltpu.VMEM((tm, LANES), F32),
            pltpu.VMEM((tm, SSM_CONV_DIM), F32),
            pltpu.VMEM((tm, BR_TOTAL), BF16),
        ],
        compiler_params=pltpu.CompilerParams(
            dimension_semantics=("arbitrary",), vmem_limit_bytes=_vmem_limit(est)),
        name="mixers",
    )(x, *consts)


def _ffn_kernel(x_ref, g_ref, wg_ref, wu_ref, wd_ref, o_ref):
    x = x_ref[...]
    h = _rms(x, g_ref[...]).astype(BF16)
    acc = x
    for c in range(wg_ref.shape[1] // FFN_CHUNK):
        sl = slice(c * FFN_CHUNK, (c + 1) * FFN_CHUNK)
        gate = jnp.dot(h, wg_ref[:, sl], preferred_element_type=F32)
        up = jnp.dot(h, wu_ref[:, sl], preferred_element_type=F32)
        acc = acc + jnp.dot((_silu(gate) * up).astype(BF16), wd_ref[sl, :], preferred_element_type=F32)
    o_ref[...] = acc


def _dense_ffn(x, g, wg, wu, wd):
    L, d = x.shape
    f = wg.shape[1]
    tm = TM_FFN
    est = 4 * tm * d * 4 + 3 * d * f * 2 + 4 * tm * FFN_CHUNK * 4
    return pl.pallas_call(
        _ffn_kernel,
        grid=(L // tm,),
        in_specs=[pl.BlockSpec((tm, d), lambda i: (i, 0)),
                  pl.BlockSpec((1, d), lambda i: (0, 0)),
                  pl.BlockSpec((d, f), lambda i: (0, 0), pipeline_mode=pl.Buffered(1)),
                  pl.BlockSpec((d, f), lambda i: (0, 0), pipeline_mode=pl.Buffered(1)),
                  pl.BlockSpec((f, d), lambda i: (0, 0), pipeline_mode=pl.Buffered(1))],
        out_specs=pl.BlockSpec((tm, d), lambda i: (i, 0)),
        out_shape=jax.ShapeDtypeStruct((L, d), F32),
        compiler_params=pltpu.CompilerParams(
            dimension_semantics=("parallel",), vmem_limit_bytes=_vmem_limit(est)),
        name="dense_swiglu",
    )(x, g, wg, wu, wd)


def _router_kernel(x_ref, g_ref, rw_ref, ri_ref, rg_ref, cnt_ref, carry):
    i = pl.program_id(0)
    tm = x_ref.shape[0]

    @pl.when(i == 0)
    def _():
        carry[...] = jnp.zeros_like(carry)

    h = _rms(x_ref[...], g_ref[...])
    pieces = jnp.dot(jnp.concatenate(_split3(h), axis=0), rw_ref[...], preferred_element_type=F32)
    s3 = pieces[0:tm] + pieces[tm:2 * tm] + pieces[2 * tm:3 * tm]
    logits = s3 + pltpu.roll(s3, LANES - N_EXPERTS, 1) + pltpu.roll(s3, LANES - 2 * N_EXPERTS, 1)
    er = ROUTER_ROWS
    expert = lax.broadcasted_iota(jnp.int32, (er, 1), 0).astype(F32)
    lt = jnp.where(expert < N_EXPERTS, logits.T[0:er, :], -jnp.inf)
    m1 = jnp.max(lt, axis=0, keepdims=True)
    i1 = jnp.min(jnp.where(lt == m1, expert, float(er)), axis=0, keepdims=True)
    rest = jnp.where(expert == i1, -jnp.inf, lt)
    m2 = jnp.max(rest, axis=0, keepdims=True)
    i2 = jnp.min(jnp.where(rest == m2, expert, float(er)), axis=0, keepdims=True)
    e21 = jnp.exp(m2 - m1)
    g1 = 1.0 / (1.0 + e21)
    g2 = e21 / (1.0 + e21)

    hot1 = expert == i1
    hot2 = expert == i2
    onehot = jnp.logical_or(hot1, hot2)
    row = lax.broadcasted_iota(jnp.int32, (tm, 1), 0)
    colt = lax.broadcasted_iota(jnp.int32, (1, tm), 1)
    earlier = (row < colt).astype(BF16)
    seen = jnp.dot(onehot.astype(BF16), earlier, preferred_element_type=F32) + carry[...]
    r1 = jnp.sum(jnp.where(hot1, seen, 0.0), axis=0, keepdims=True)
    r2 = jnp.sum(jnp.where(hot2, seen, 0.0), axis=0, keepdims=True)
    carry[...] = carry[...] + jnp.sum(onehot.astype(F32), axis=1, keepdims=True)

    field = lax.broadcasted_iota(jnp.int32, (LANES, 1), 0)
    ri_t = jnp.where(field == 0, i1, jnp.where(field == 1, i2, jnp.where(field == 2, r1, r2)))
    ri_ref[...] = ri_t.T.astype(jnp.int32)
    rg_ref[...] = jnp.where(field == 0, g1, g2).T
    cnt_ref[...] = jnp.broadcast_to(carry[...], cnt_ref.shape)


def _router(x, g, rw):
    L, d = x.shape
    tm = TM_ROUTER
    return pl.pallas_call(
        _router_kernel,
        grid=(L // tm,),
        in_specs=[pl.BlockSpec((tm, d), lambda i: (i, 0)),
                  pl.BlockSpec((1, d), lambda i: (0, 0)),
                  pl.BlockSpec((d, LANES), lambda i: (0, 0))],
        out_specs=[pl.BlockSpec((tm, LANES), lambda i: (i, 0)),
                   pl.BlockSpec((tm, LANES), lambda i: (i, 0)),
                   pl.BlockSpec((ROUTER_ROWS, LANES), lambda i: (0, 0))],
        out_shape=[jax.ShapeDtypeStruct((L, LANES), jnp.int32),
                   jax.ShapeDtypeStruct((L, LANES), F32),
                   jax.ShapeDtypeStruct((ROUTER_ROWS, LANES), F32)],
        scratch_shapes=[pltpu.VMEM((ROUTER_ROWS, 1), F32)],
        compiler_params=pltpu.CompilerParams(dimension_semantics=("arbitrary",)),
        name="router_top2",
    )(x, g, rw)


def _row_copy(src, dst, sem, s, d):
    return pltpu.make_async_copy(src.at[pl.ds(s, 1), :], dst.at[pl.ds(d, 1), :], sem)


def _start_row_copies(tm, copy_of):
    def start(t, c):
        for k in range(TOP_K):
            copy_of(t, k).start(priority=k)
        return c

    lax.fori_loop(0, tm, start, 0, unroll=DMA_UNROLL)


def _wait_row_copies(tm, copy_of):
    def wait(t, c):
        for k in range(TOP_K):
            copy_of(t, k).wait()
        return c

    lax.fori_loop(0, tm, wait, 0, unroll=DMA_UNROLL)


def _dispatch_kernel(pos_ref, pos_prev_ref, tail_ref, x_ref, g_ref, xs_ref, h_s, sems):
    j = pl.program_id(0)
    tm = x_ref.shape[0] // 2

    def copies(pref, half):
        return lambda t, k: _row_copy(h_s.at[half], xs_ref, sems.at[half], t,
                                      pref[0, 0, TOP_K * (half * tm + t) + k])

    @pl.when(j == 0)
    def _clear_unwritten_tiles():
        h_s[0] = jnp.zeros((tm, h_s.shape[2]), F32)
        n_tiles = xs_ref.shape[0] // TM_GROUP
        tiles = [(tail_ref[e] >= 0, tail_ref[e]) for e in range(N_EXPERTS)]
        tiles += [(t >= tail_ref[N_EXPERTS], t * TM_GROUP) for t in range(n_tiles - N_EXPERTS, n_tiles)]

        def clear_copy(first_row, part):
            first = first_row + part * tm
            if not isinstance(first, int):
                first = pl.multiple_of(first, tm)
            return pltpu.make_async_copy(h_s.at[0], xs_ref.at[pl.ds(first, tm), :], sems.at[0])

        for action in ("start", "wait"):
            for needed, first_row in tiles:
                @pl.when(needed)
                def _(first_row=first_row, action=action):
                    for part in range(TM_GROUP // tm):
                        getattr(clear_copy(first_row, part), action)()

    h_s[0] = _rms(x_ref[0:tm, :], g_ref[...])
    _start_row_copies(tm, copies(pos_ref, 0))

    @pl.when(j > 0)
    def _():
        _wait_row_copies(tm, copies(pos_prev_ref, 1))

    h_s[1] = _rms(x_ref[tm:2 * tm, :], g_ref[...])
    _start_row_copies(tm, copies(pos_ref, 1))
    _wait_row_copies(tm, copies(pos_ref, 0))

    @pl.when(j == pl.num_programs(0) - 1)
    def _():
        _wait_row_copies(tm, copies(pos_ref, 1))


def _dispatch(x, g, pos, tail_rows, n_sorted):
    L, d = x.shape
    tm = TM_ROUTE
    assert TM_GROUP % tm == 0 and L % (2 * tm) == 0
    pos3 = pos.reshape(L // (2 * tm), 1, 2 * TOP_K * tm)
    return pl.pallas_call(
        _dispatch_kernel,
        grid=(L // (2 * tm),),
        in_specs=[pl.BlockSpec((1, 1, 2 * TOP_K * tm), lambda j: (j, 0, 0), memory_space=pltpu.SMEM),
                  pl.BlockSpec((1, 1, 2 * TOP_K * tm), lambda j: (jnp.maximum(j - 1, 0), 0, 0),
                               memory_space=pltpu.SMEM),
                  pl.BlockSpec(memory_space=pltpu.SMEM),
                  pl.BlockSpec((2 * tm, d), lambda j: (j, 0)),
                  pl.BlockSpec((1, d), lambda j: (0, 0))],
        out_specs=pl.BlockSpec(memory_space=pl.ANY),
        out_shape=jax.ShapeDtypeStruct((n_sorted, d), F32),
        scratch_shapes=[pltpu.VMEM((2, tm, d), F32), pltpu.SemaphoreType.DMA((2,))],
        compiler_params=pltpu.CompilerParams(dimension_semantics=("arbitrary",)),
        name="moe_dispatch",
    )(pos3, pos3, tail_rows, x, g)


def _expert_kernel(te_ref, tv_ref, tb_ref, xs_ref, wg_hbm, wu_hbm, wd_hbm, ys_ref,
                   wg_s, wu_s, wd_s, stage_up, stage_down, sems, *, moe_layer):
    del tb_ref
    i = pl.program_id(0)
    e = te_ref[i]
    first_tile_of_expert = jnp.logical_or(i == 0, e != te_ref[jnp.maximum(i - 1, 0)])

    @pl.when(jnp.logical_and(tv_ref[i] != 0, first_tile_of_expert))
    def _load_expert_weights():
        slots = stage_up.shape[0]
        rows_up, rows_down = stage_up.shape[1], stage_down.shape[1]
        steps = ([(wg_hbm, wg_s, stage_up, r) for r in range(0, wg_s.shape[0], rows_up)]
                 + [(wu_hbm, wu_s, stage_up, r) for r in range(0, wu_s.shape[0], rows_up)]
                 + [(wd_hbm, wd_s, stage_down, r) for r in range(0, wd_s.shape[0], rows_down)])

        def copy(k):
            src, _, stage, r = steps[k]
            return pltpu.make_async_copy(src.at[moe_layer, e, pl.ds(r, stage.shape[1]), :],
                                         stage.at[k % slots], sems.at[k % slots])

        for k in range(min(slots - 1, len(steps))):
            copy(k).start()
        for k, (_, dst, stage, r) in enumerate(steps):
            if k + slots - 1 < len(steps):
                copy(k + slots - 1).start()
            copy(k).wait()
            dst[r:r + stage.shape[1], :] = stage[k % slots].astype(BF16)

    @pl.when(tv_ref[i] == 0)
    def _():
        ys_ref[...] = jnp.zeros_like(ys_ref)

    @pl.when(tv_ref[i] != 0)
    def _():
        x = xs_ref[...].astype(BF16)
        acc = None
        for c in range(wg_s.shape[1] // MOE_CHUNK):
            sl = slice(c * MOE_CHUNK, (c + 1) * MOE_CHUNK)
            gate = jnp.dot(x, wg_s[:, sl], preferred_element_type=F32)
            up = jnp.dot(x, wu_s[:, sl], preferred_element_type=F32)
            part = jnp.dot((_silu(gate) * up).astype(BF16), wd_s[sl, :], preferred_element_type=F32)
            acc = part if acc is None else acc + part
        ys_ref[...] = acc


def _experts(xs, tile_expert, tile_valid, tile_block, wg, wu, wd, moe_layer):
    n_sorted, d = xs.shape
    f = wg.shape[3]
    tm = TM_GROUP
    assert d % W_LOAD_ROWS_UP == 0 and f % W_LOAD_ROWS_DOWN == 0 and f % MOE_CHUNK == 0
    est = (3 * d * f * 2 + W_LOAD_SLOTS * (W_LOAD_ROWS_UP * f + W_LOAD_ROWS_DOWN * d) * 4
           + 4 * tm * d * 4 + 4 * tm * MOE_CHUNK * 4)
    return pl.pallas_call(
        functools.partial(_expert_kernel, moe_layer=moe_layer),
        grid_spec=pltpu.PrefetchScalarGridSpec(
            num_scalar_prefetch=3,
            grid=(n_sorted // tm,),
            in_specs=[pl.BlockSpec((tm, d), lambda i, te, tv, tb: (tb[i], 0)),
                      pl.BlockSpec(memory_space=pl.ANY),
                      pl.BlockSpec(memory_space=pl.ANY),
                      pl.BlockSpec(memory_space=pl.ANY)],
            out_specs=pl.BlockSpec((tm, d), lambda i, te, tv, tb: (i, 0)),
            scratch_shapes=[pltpu.VMEM((d, f), BF16), pltpu.VMEM((d, f), BF16), pltpu.VMEM((f, d), BF16),
                            pltpu.VMEM((W_LOAD_SLOTS, W_LOAD_ROWS_UP, f), F32),
                            pltpu.VMEM((W_LOAD_SLOTS, W_LOAD_ROWS_DOWN, d), F32),
                            pltpu.SemaphoreType.DMA((W_LOAD_SLOTS,))],
        ),
        out_shape=jax.ShapeDtypeStruct((n_sorted, d), F32),
        compiler_params=pltpu.CompilerParams(
            dimension_semantics=("arbitrary",), vmem_limit_bytes=_vmem_limit(est)),
        name="expert_swiglu",
    )(tile_expert, tile_valid, tile_block, xs, wg, wu, wd)


def _combine_kernel(pos_ref, pos_next_ref, x_ref, rg_ref, fg_ref, ys_ref, o_ref, buf, sems):
    j = pl.program_id(0)
    tm = x_ref.shape[0] // 2

    def copies(pref, half):
        return lambda t, k: _row_copy(ys_ref, buf.at[half, k], sems.at[half],
                                      pref[0, 0, TOP_K * (half * tm + t) + k], t)

    def combine(half):
        rows = slice(half * tm, (half + 1) * tm)
        rg = rg_ref[rows, :]
        y = x_ref[rows, :] + rg[:, 0:1] * buf[half, 0] + rg[:, 1:2] * buf[half, 1]
        o_ref[rows, :] = _rms(y, fg_ref[...])

    @pl.when(j == 0)
    def _():
        _start_row_copies(tm, copies(pos_ref, 0))

    _start_row_copies(tm, copies(pos_ref, 1))
    _wait_row_copies(tm, copies(pos_ref, 0))
    combine(0)

    @pl.when(j < pl.num_programs(0) - 1)
    def _():
        _start_row_copies(tm, copies(pos_next_ref, 0))

    _wait_row_copies(tm, copies(pos_ref, 1))
    combine(1)


def _combine(x, rg, pos, ys, final_g):
    L, d = x.shape
    tm = TM_ROUTE
    n = L // (2 * tm)
    pos3 = pos.reshape(n, 1, 2 * TOP_K * tm)
    return pl.pallas_call(
        _combine_kernel,
        grid=(n,),
        in_specs=[pl.BlockSpec((1, 1, 2 * TOP_K * tm), lambda j: (j, 0, 0), memory_space=pltpu.SMEM),
                  pl.BlockSpec((1, 1, 2 * TOP_K * tm), lambda j: (jnp.minimum(j + 1, n - 1), 0, 0),
                               memory_space=pltpu.SMEM),
                  pl.BlockSpec((2 * tm, d), lambda j: (j, 0)),
                  pl.BlockSpec((2 * tm, LANES), lambda j: (j, 0)),
                  pl.BlockSpec((1, d), lambda j: (0, 0)),
                  pl.BlockSpec(memory_space=pl.ANY)],
        out_specs=pl.BlockSpec((2 * tm, d), lambda j: (j, 0)),
        out_shape=jax.ShapeDtypeStruct((L, d), F32),
        scratch_shapes=[pltpu.VMEM((2, TOP_K, tm, d), F32), pltpu.SemaphoreType.DMA((2,))],
        compiler_params=pltpu.CompilerParams(dimension_semantics=("arbitrary",)),
        name="moe_combine_norm",
    )(pos3, pos3, x, rg, final_g, ys)


def _moe_ffn_final_norm(x, g2, router_w, wg, wu, wd, moe_layer, final_g):
    L, d = x.shape
    rw_hi = router_w.astype(BF16)
    rw_mid = (router_w - rw_hi.astype(F32)).astype(BF16)
    rw_lo = (router_w - rw_hi.astype(F32) - rw_mid.astype(F32)).astype(BF16)
    rw = jnp.pad(jnp.concatenate([rw_hi, rw_mid, rw_lo], axis=1), ((0, 0), (0, LANES - 3 * N_EXPERTS)))
    ri, rg, cnt = _router(x, g2, rw)
    counts = cnt[:N_EXPERTS, 0].astype(jnp.int32)
    padded = ((counts + TM_GROUP - 1) // TM_GROUP) * TM_GROUP
    ends = jnp.cumsum(padded)
    starts = ends - padded
    chosen = ri[:, 0:TOP_K, None] == jnp.arange(N_EXPERTS, dtype=jnp.int32)
    pos = jnp.sum(jnp.where(chosen, starts, 0), axis=-1) + ri[:, TOP_K:2 * TOP_K]
    n_sorted = TOP_K * L + N_EXPERTS * TM_GROUP
    tile_row = jnp.arange(n_sorted // TM_GROUP, dtype=jnp.int32) * TM_GROUP
    tile_valid = (tile_row < ends[-1]).astype(jnp.int32)
    last_row = jnp.minimum(tile_row, ends[-1] - TM_GROUP)
    tile_expert = jnp.sum((ends[None, :] <= last_row[:, None]).astype(jnp.int32), axis=1)
    tile_expert = jnp.minimum(tile_expert, N_EXPERTS - 1)
    tile_block = last_row // TM_GROUP
    tail_rows = jnp.concatenate(
        [jnp.where(padded > 0, ends - TM_GROUP, -1), ends[-1:] // TM_GROUP]).astype(jnp.int32)

    xs = _dispatch(x, g2, pos, tail_rows, n_sorted)
    ys = _experts(xs, tile_expert, tile_valid, tile_block, wg, wu, wd, moe_layer)
    return _combine(x, rg, pos, ys, final_g.reshape(1, d))


def kernel(x, norm1_g, w_in, w_gate, sc_conv_w, ssm_conv_w, ssm_conv_b, ssm_dt_bias, ssm_a_log, ssm_d, ssm_norm_g, gm_ln_g, gm_ln_b, gm_ws, gm_bias, pool_map, pool_scale, w_br_a, w_br_b, w_br_c, w_br_d, w_out, norm2_g, ffn_wg, ffn_wu, ffn_wd, router_w, moe_wg, moe_wu, moe_wd, final_g):
    bsz, L, d = x.shape
    assert bsz == 1 and d == D_MODEL and L % TM_FFN == 0
    depth = norm1_g.shape[0]
    assert depth == 2, "layer 0 uses the dense SwiGLU, layer 1 the expert SwiGLU followed by the final norm"
    xt = x.reshape(L, d)
    rexp = _head_expand_matrix()
    pad_h = LANES - SSM_HEADS
    for layer in range(depth):
        w_in_l = w_in[layer]
        p = {
            "norm1": norm1_g[layer].reshape(1, d),
            "w_in": w_in_l[:, :P_DT].astype(BF16),
            "w_in_dt": jnp.pad(w_in_l[:, P_DT:IN_DT_END], ((0, 0), (0, pad_h))).astype(BF16),
            "w_in_tail": w_in_l[:, IN_DT_END:].astype(BF16),
            "w_gate": w_gate[layer].astype(BF16),
            "scw": sc_conv_w[layer],
            "mcw": ssm_conv_w[layer],
            "mcb": ssm_conv_b[layer].reshape(1, -1),
            "dtb": jnp.pad(ssm_dt_bias[layer], (0, pad_h)).reshape(1, LANES),
            "alog": jnp.pad(ssm_a_log[layer], (0, pad_h)).reshape(1, LANES),
            "dskip": jnp.repeat(ssm_d[layer], SSM_HEAD_DIM).reshape(1, SSM_D_INNER),
            "ng": ssm_norm_g[layer].reshape(1, -1),
            "lng": gm_ln_g[layer].reshape(1, -1),
            "lnb": gm_ln_b[layer].reshape(1, -1),
            "gws": gm_ws[layer],
            "gbias": jnp.repeat(gm_bias[layer].T, GM_WIDTH // GM_GROUPS, axis=1),
            "pmap": pool_map[layer].astype(BF16),
            "pscale": pool_scale[layer].reshape(1, -1),
            "rexp": rexp,
            "wbr_a": w_br_a[layer].astype(BF16),
            "wbr_b": w_br_b[layer].astype(BF16),
            "wbr_c": w_br_c[layer].astype(BF16),
            "wbr_d": w_br_d[layer].astype(BF16),
            "wout": w_out[layer].astype(BF16),
        }
        xt = _mixer(xt, p)
        idx = layer // 2
        g2 = norm2_g[layer].reshape(1, d)
        if layer % 2 == 0:
            xt = _dense_ffn(xt, g2, ffn_wg[idx].astype(BF16), ffn_wu[idx].astype(BF16), ffn_wd[idx].astype(BF16))
        else:
            xt = _moe_ffn_final_norm(xt, g2, router_w[idx], moe_wg, moe_wu, moe_wd, idx, final_g)
    return xt.reshape(bsz, L, d)
```
